```python
import math
import jax, jax.numpy as jnp
from jax import lax
import numpy as np

D_MODEL = 1024
BATCH = 8
SEQ = 4096
DEPTH = 2

N_MIXERS = 2
N_A = (DEPTH + 1) // 2
N_B = DEPTH // 2
D_FF = 2816
LRU_WIDTH = D_MODEL
LRU_HEADS = 4
LRU_BLOCK = LRU_WIDTH // LRU_HEADS
CONV_WIDTH = 4
LRU_C = 8.0
HEAD_DIM = 64
N_Q_HEADS = D_MODEL // HEAD_DIM
N_KV_HEADS = 2
Q_PER_KV = N_Q_HEADS // N_KV_HEADS
WINDOW = 128
ATTN_BLOCK = WINDOW
QKV_WIDTH = (N_Q_HEADS + 2 * N_KV_HEADS) * HEAD_DIM
RMS_EPS = 1e-6
NEG_INF = -1e30

kernel_name = "hybrid_rglru_swa_sink_macaron"


def rms_norm(x, g):
    xf = x.astype(jnp.float32)
    y = xf * lax.rsqrt(jnp.mean(xf * xf, axis=-1, keepdims=True) + RMS_EPS)
    return (y * g.astype(jnp.float32)).astype(x.dtype)


def swiglu(x, w_gate, w_up, w_down):
    return (jax.nn.silu(x @ w_gate) * (x @ w_up)) @ w_down


def causal_depthwise_conv(x, w, b):
    s = x.shape[1]
    xp = jnp.pad(x, ((0, 0), (CONV_WIDTH - 1, 0), (0, 0)))
    out = b
    for k in range(CONV_WIDTH):
        out = out + xp[:, k:k + s] * w[k]
    return out


def block_diag_linear(x, w, b):
    bsz, s, _ = x.shape
    xh = x.reshape(bsz, s, LRU_HEADS, LRU_BLOCK)
    y = jnp.einsum('bshi,hij->bshj', xh, w).reshape(bsz, s, LRU_WIDTH)
    return y + b


def _lin_combine(c1, c2):
    a1, b1 = c1
    a2, b2 = c2
    return a1 * a2, a2 * b1 + b2


def rglru_mixer(x, w_in, b_in, conv_w, conv_b, w_a, b_a, w_i, b_i, lam, w_out, b_out):
    proj = x @ w_in + b_in
    y_branch = jax.nn.gelu(proj[..., :LRU_WIDTH])
    xb = causal_depthwise_conv(proj[..., LRU_WIDTH:], conv_w, conv_b)
    r = jax.nn.sigmoid(block_diag_linear(xb, w_a, b_a).astype(jnp.float32))
    i_gate = jax.nn.sigmoid(block_diag_linear(xb, w_i, b_i).astype(jnp.float32))
    log_a = LRU_C * r * jax.nn.log_sigmoid(lam.astype(jnp.float32))
    a = jnp.exp(log_a)
    mult = jnp.sqrt(-jnp.expm1(2.0 * log_a))
    u = mult * (i_gate * xb.astype(jnp.float32))
    _, h = lax.associative_scan(_lin_combine, (a, u), axis=1)
    return (h.astype(x.dtype) * y_branch) @ w_out + b_out


def swa_sink_attention(x, w_qkv, b_qkv, sinks, w_o, b_o):
    bsz, s, _ = x.shape
    nb = s // ATTN_BLOCK
    qkv = x @ w_qkv + b_qkv
    qd = N_Q_HEADS * HEAD_DIM
    kd = N_KV_HEADS * HEAD_DIM
    q = qkv[..., :qd].reshape(bsz, nb, ATTN_BLOCK, N_KV_HEADS, Q_PER_KV, HEAD_DIM)
    k = qkv[..., qd:qd + kd].reshape(bsz, nb, ATTN_BLOCK, N_KV_HEADS, HEAD_DIM)
    v = qkv[..., qd + kd:].reshape(bsz, nb, ATTN_BLOCK, N_KV_HEADS, HEAD_DIM)
    k_prev = jnp.concatenate([jnp.zeros_like(k[:, :1]), k[:, :-1]], axis=1)
    v_prev = jnp.concatenate([jnp.zeros_like(v[:, :1]), v[:, :-1]], axis=1)
    kk = jnp.concatenate([k_prev, k], axis=2)
    vv = jnp.concatenate([v_prev, v], axis=2)
    scale = 1.0 / math.sqrt(HEAD_DIM)
    scores = jnp.einsum('bnqkgd,bnskd->bnkgqs', q, kk).astype(jnp.float32) * scale
    qi = jnp.arange(ATTN_BLOCK)[:, None]
    kj = jnp.arange(2 * ATTN_BLOCK)[None, :]
    diff = qi + ATTN_BLOCK - kj
    band = (diff >= 0) & (diff < WINDOW)
    blk_idx = jnp.arange(nb)[:, None, None]
    valid = band[None] & ((blk_idx > 0) | (kj[None] >= ATTN_BLOCK))
    scores = jnp.where(valid[None, :, None, None], scores, NEG_INF)
    sink = jnp.broadcast_to(
        sinks.astype(jnp.float32).reshape(1, 1, N_KV_HEADS, Q_PER_KV, 1, 1),
        scores.shape[:-1] + (1,))
    probs = jax.nn.softmax(jnp.concatenate([scores, sink], axis=-1), axis=-1)[..., :-1]
    out = jnp.einsum('bnkgqs,bnskd->bnqkgd', probs.astype(x.dtype), vv)
    return out.reshape(bsz, s, qd) @ w_o + b_o


def setup_inputs(seed: int = 0) -> dict:
    key = jax.random.key(seed)
    ks = iter(jax.random.split(key, 40))
    f32 = jnp.float32

    def nrm(shape, fan_in):
        return jax.random.normal(next(ks), shape, f32) * (fan_in ** -0.5)

    def gain(shape):
        return 1.0 + 0.05 * jax.random.normal(next(ks), shape, f32)

    def bias(shape):
        return 0.01 * jax.random.normal(next(ks), shape, f32)

    x = jax.random.normal(next(ks), (BATCH, SEQ, D_MODEL), f32)
    norm_ffn = gain((DEPTH, 2, 2, D_MODEL))
    norm_mix = gain((DEPTH, 2, D_MODEL))
    ffn_w_gate = nrm((DEPTH, 2, D_MODEL, D_FF), D_MODEL)
    ffn_w_up = nrm((DEPTH, 2, D_MODEL, D_FF), D_MODEL)
    ffn_w_down = nrm((DEPTH, 2, D_FF, D_MODEL), D_FF)
    lru_w_in = nrm((N_A, D_MODEL, 2 * LRU_WIDTH), D_MODEL)
    lru_b_in = bias((N_A, 2 * LRU_WIDTH))
    lru_conv_w = nrm((N_A, CONV_WIDTH, LRU_WIDTH), CONV_WIDTH)
    lru_conv_b = bias((N_A, LRU_WIDTH))
    lru_w_a = nrm((N_A, LRU_HEADS, LRU_BLOCK, LRU_BLOCK), LRU_BLOCK)
    lru_b_a = bias((N_A, LRU_WIDTH))
    lru_w_i = nrm((N_A, LRU_HEADS, LRU_BLOCK, LRU_BLOCK), LRU_BLOCK)
    lru_b_i = bias((N_A, LRU_WIDTH))
    a0 = jax.random.uniform(next(ks), (N_A, LRU_WIDTH), f32, 0.81, 0.998)
    base = a0 ** (1.0 / LRU_C)
    lru_lambda = jnp.log(base) - jnp.log1p(-base)
    lru_w_out = nrm((N_A, LRU_WIDTH, D_MODEL), LRU_WIDTH)
    lru_b_out = bias((N_A, D_MODEL))
    attn_w_qkv = nrm((N_B, D_MODEL, QKV_WIDTH), D_MODEL)
    attn_b_qkv = bias((N_B, QKV_WIDTH))
    attn_sinks = jax.random.normal(next(ks), (N_B, N_Q_HEADS), f32)
    attn_w_o = nrm((N_B, N_Q_HEADS * HEAD_DIM, D_MODEL), N_Q_HEADS * HEAD_DIM)
    attn_b_o = bias((N_B, D_MODEL))
    return {"x": x, "norm_ffn": norm_ffn, "norm_mix": norm_mix,
            "ffn_w_gate": ffn_w_gate, "ffn_w_up": ffn_w_up, "ffn_w_down": ffn_w_down,
            "lru_w_in": lru_w_in, "lru_b_in": lru_b_in, "lru_conv_w": lru_conv_w,
            "lru_conv_b": lru_conv_b, "lru_w_a": lru_w_a, "lru_b_a": lru_b_a,
            "lru_w_i": lru_w_i, "lru_b_i": lru_b_i, "lru_lambda": lru_lambda,
            "lru_w_out": lru_w_out, "lru_b_out": lru_b_out,
            "attn_w_qkv": attn_w_qkv, "attn_b_qkv": attn_b_qkv, "attn_sinks": attn_sinks,
            "attn_w_o": attn_w_o, "attn_b_o": attn_b_o}


def reference(x, norm_ffn, norm_mix, ffn_w_gate, ffn_w_up, ffn_w_down,
              lru_w_in, lru_b_in, lru_conv_w, lru_conv_b, lru_w_a, lru_b_a,
              lru_w_i, lru_b_i, lru_lambda, lru_w_out, lru_b_out,
              attn_w_qkv, attn_b_qkv, attn_sinks, attn_w_o, attn_b_o):
    h = x
    for layer in range(DEPTH):
        f = swiglu(rms_norm(h, norm_ffn[layer, 0, 0]),
                   ffn_w_gate[layer, 0], ffn_w_up[layer, 0], ffn_w_down[layer, 0])
        h = h + 0.5 * rms_norm(f, norm_ffn[layer, 0, 1])
        hn = rms_norm(h, norm_mix[layer, 0])
        j = layer // N_MIXERS
        if layer % N_MIXERS == 0:
            m = rglru_mixer(hn, lru_w_in[j], lru_b_in[j], lru_conv_w[j], lru_conv_b[j],
                            lru_w_a[j], lru_b_a[j], lru_w_i[j], lru_b_i[j],
                            lru_lambda[j], lru_w_out[j], lru_b_out[j])
        else:
            m = swa_sink_attention(hn, attn_w_qkv[j], attn_b_qkv[j], attn_sinks[j],
                                   attn_w_o[j], attn_b_o[j])
        h = h + rms_norm(m, norm_mix[layer, 1])
        f = swiglu(rms_norm(h, norm_ffn[layer, 1, 0]),
                   ffn_w_gate[layer, 1], ffn_w_up[layer, 1], ffn_w_down[layer, 1])
        h = h + 0.5 * rms_norm(f, norm_ffn[layer, 1, 1])
    return h
```

```python
import functools
import math

import jax
import jax.numpy as jnp
from jax import lax
from jax.experimental import pallas as pl
from jax.experimental.pallas import tpu as pltpu

F32 = jnp.float32
BF16 = jnp.bfloat16

RMS_EPS = 1e-6
NEG_INF = -1e30
LRU_C = 8.0
LRU_HEADS = 4
CONV_WIDTH = 4
HEAD_DIM = 64
N_KV_HEADS = 2
Q_PER_KV = 8
ATTN_BLOCK = 128

V7X_SUBLANES = 8
V7X_LANES = 128
V7X_MXU_DIM = 256
V7X_VMEM_LIMIT_BYTES = 56 * 1024 * 1024

FFN_ROWS = 512
FFN_CHUNK = V7X_MXU_DIM
MIX_ROWS = 512


def _rms(x, g):
    ms = jnp.mean(x * x, axis=-1, keepdims=True)
    return x * lax.rsqrt(ms + RMS_EPS) * g


def _dot(a, b):
    return jnp.dot(a, b, preferred_element_type=F32)


def _const_spec(shape):
    nd = len(shape)
    return pl.BlockSpec(shape, lambda *_: (0,) * nd)


def _ffn_kernel(x_ref, gin_ref, gout_ref, wg_ref, wu_ref, wd_ref, o_ref, acc_ref):
    x = x_ref[...]
    xn = _rms(x, gin_ref[...]).astype(BF16)
    d_ff = wg_ref.shape[1]
    for c in range(d_ff // FFN_CHUNK):
        sl = slice(c * FFN_CHUNK, (c + 1) * FFN_CHUNK)
        g = _dot(xn, wg_ref[:, sl])
        u = _dot(xn, wu_ref[:, sl])
        mid = (g * jax.nn.sigmoid(g) * u).astype(BF16)
        d = _dot(mid, wd_ref[sl, :])
        if c == 0:
            acc_ref[...] = d
        else:
            acc_ref[...] += d
    o_ref[...] = x + 0.5 * _rms(acc_ref[...], gout_ref[...])


def _ffn_block(h, g_in, g_out, w_gate, w_up, w_down):
    n, d = h.shape
    d_ff = w_gate.shape[1]
    assert n % FFN_ROWS == 0 and d_ff % FFN_CHUNK == 0
    row_spec = pl.BlockSpec((FFN_ROWS, d), lambda i: (i, 0))
    resident = functools.partial(pl.BlockSpec, pipeline_mode=pl.Buffered(1))
    return pl.pallas_call(
        _ffn_kernel,
        name="ffn",
        grid=(n // FFN_ROWS,),
        in_specs=[
            row_spec,
            _const_spec((1, d)),
            _const_spec((1, d)),
            resident((d, d_ff), lambda i: (0, 0)),
            resident((d, d_ff), lambda i: (0, 0)),
            resident((d_ff, d), lambda i: (0, 0)),
        ],
        out_specs=row_spec,
        out_shape=jax.ShapeDtypeStruct((n, d), F32),
        scratch_shapes=[pltpu.VMEM((FFN_ROWS, d), F32)],
        compiler_params=pltpu.CompilerParams(
            dimension_semantics=("arbitrary",),
            vmem_limit_bytes=V7X_VMEM_LIMIT_BYTES),
    )(h, g_in.reshape(1, d), g_out.reshape(1, d),
      w_gate.astype(BF16), w_up.astype(BF16), w_down.astype(BF16))


def _scan_step(a, b, shift, axis, idx):
    keep = idx >= shift
    a_prev = pltpu.roll(a, shift, axis=axis)
    b_prev = pltpu.roll(b, shift, axis=axis)
    b = jnp.where(keep, a * b_prev, 0.0) + b
    a = jnp.where(keep, a * a_prev, a)
    return a, b


def _scan_strip(a_ref, b_ref, s_ref, hcar_ref):
    rows, lanes = a_ref.shape
    groups = rows // V7X_SUBLANES
    a3 = a_ref[...].reshape(groups, V7X_SUBLANES, lanes)
    b3 = b_ref[...].reshape(groups, V7X_SUBLANES, lanes)
    sub = lax.broadcasted_iota(jnp.int32, a3.shape, 1)
    shift = 1
    while shift < V7X_SUBLANES:
        a3, b3 = _scan_step(a3, b3, shift, 1, sub)
        shift *= 2
    a_ref[...] = a3.reshape(rows, lanes)
    b_ref[...] = b3.reshape(rows, lanes)
    ag = a_ref[pl.ds(V7X_SUBLANES - 1, groups, stride=V7X_SUBLANES), :]
    bg = b_ref[pl.ds(V7X_SUBLANES - 1, groups, stride=V7X_SUBLANES), :]
    gidx = lax.broadcasted_iota(jnp.int32, ag.shape, 0)
    shift = 1
    while shift < groups:
        ag, bg = _scan_step(ag, bg, shift, 0, gidx)
        shift *= 2
    h_in = hcar_ref[...]
    state = ag * h_in + bg
    hcar_ref[...] = state[groups - 1:groups, :]
    state_prev = jnp.where(gidx >= 1, pltpu.roll(state, 1, axis=0), h_in)
    for r in range(V7X_SUBLANES):
        s_ref[pl.ds(r, groups, stride=V7X_SUBLANES), :] = state_prev
    return a_ref[...] * s_ref[...] + b_ref[...]


def _lru_kernel(x_ref, gin_ref, gout_ref, win_ref, bin_ref, cw_ref, cb_ref, wai_ref,
                bai_ref, lam_ref, wout_ref, bout_ref, o_ref,
                ext_ref, hcar_ref, a_ref, b_ref, s_ref):
    strips, rows, _ = a_ref.shape
    width = strips * V7X_LANES
    blk = width // LRU_HEADS
    halo = V7X_SUBLANES

    @pl.when(pl.program_id(1) == 0)
    def _():
        ext_ref[0:halo, :] = jnp.zeros((halo, width), F32)
        hcar_ref[...] = jnp.zeros_like(hcar_ref)

    x = x_ref[...]
    xn = _rms(x, gin_ref[...]).astype(BF16)
    proj = _dot(xn, win_ref[...]) + bin_ref[...]
    y = jax.nn.gelu(proj[:, :width], approximate=True)

    xc = proj[:, width:]
    ext_ref[halo:halo + rows, :] = xc
    xb = cb_ref[...] + cw_ref[CONV_WIDTH - 1:CONV_WIDTH, :] * xc
    for k in range(CONV_WIDTH - 1):
        back = CONV_WIDTH - 1 - k
        xb = xb + cw_ref[k:k + 1, :] * ext_ref[halo - back:halo - back + rows, :]
    ext_ref[0:halo, :] = xc[rows - halo:rows, :]

    lam = lam_ref[...]
    log_sig_lam = jnp.minimum(lam, 0.0) - jnp.log(1.0 + jnp.exp(-jnp.abs(lam)))
    for h in range(LRU_HEADS):
        cs = slice(h * blk, (h + 1) * blk)
        xbh = xb[:, cs]
        gates = jax.nn.sigmoid(_dot(xbh.astype(BF16), wai_ref[h]) + bai_ref[h])
        r = gates[:, :blk]
        i_gate = gates[:, blk:]
        log_a = LRU_C * r * log_sig_lam[:, cs]
        a = jnp.exp(log_a)
        mult = jnp.sqrt(1.0 - a * a)
        b = mult * (i_gate * xbh)
        for t in range(blk // V7X_LANES):
            ls = slice(t * V7X_LANES, (t + 1) * V7X_LANES)
            a_ref[h * (blk // V7X_LANES) + t] = a[:, ls]
            b_ref[h * (blk // V7X_LANES) + t] = b[:, ls]

    hseq = jnp.concatenate(
        [_scan_strip(a_ref.at[c], b_ref.at[c], s_ref.at[c],
                     hcar_ref.at[:, c * V7X_LANES:(c + 1) * V7X_LANES])
         for c in range(width // V7X_LANES)], axis=1)

    m = _dot((hseq * y).astype(BF16), wout_ref[...]) + bout_ref[...]
    o_ref[...] = x + _rms(m, gout_ref[...])


def _lru_block(h, batch, g_in, g_out, w_in, b_in, conv_w, conv_b, w_a, b_a, w_i, b_i,
               lam, w_out, b_out):
    n, d = h.shape
    width = w_out.shape[0]
    blk = width // LRU_HEADS
    tiles = n // batch // MIX_ROWS
    assert n == batch * tiles * MIX_ROWS
    w_ai = jnp.concatenate([w_a, w_i], axis=-1).astype(BF16)
    b_ai = jnp.concatenate([b_a.reshape(LRU_HEADS, 1, blk),
                            b_i.reshape(LRU_HEADS, 1, blk)], axis=-1)
    row_spec = pl.BlockSpec((MIX_ROWS, d), lambda b, s: (b * tiles + s, 0))
    return pl.pallas_call(
        _lru_kernel,
        name="rglru",
        grid=(batch, tiles),
        in_specs=[
            row_spec,
            _const_spec((1, d)),
            _const_spec((1, d)),
            _const_spec((d, 2 * width)),
            _const_spec((1, 2 * width)),
            _const_spec((CONV_WIDTH, width)),
            _const_spec((1, width)),
            _const_spec((LRU_HEADS, blk, 2 * blk)),
            _const_spec((LRU_HEADS, 1, 2 * blk)),
            _const_spec((1, width)),
            _const_spec((width, d)),
            _const_spec((1, d)),
        ],
        out_specs=row_spec,
        out_shape=jax.ShapeDtypeStruct((n, d), F32),
        scratch_shapes=[
            pltpu.VMEM((V7X_SUBLANES + MIX_ROWS, width), F32),
            pltpu.VMEM((1, width), F32),
            pltpu.VMEM((width // V7X_LANES, MIX_ROWS, V7X_LANES), F32),
            pltpu.VMEM((width // V7X_LANES, MIX_ROWS, V7X_LANES), F32),
            pltpu.VMEM((width // V7X_LANES, MIX_ROWS, V7X_LANES), F32),
        ],
        compiler_params=pltpu.CompilerParams(
            dimension_semantics=("arbitrary", "arbitrary"),
            vmem_limit_bytes=V7X_VMEM_LIMIT_BYTES),
    )(h, g_in.reshape(1, d), g_out.reshape(1, d), w_in.astype(BF16),
      b_in.reshape(1, -1), conv_w, conv_b.reshape(1, -1), w_ai, b_ai,
      lam.reshape(1, -1), w_out.astype(BF16), b_out.reshape(1, d))


def _attn_kernel(sink_ref, x_ref, gin_ref, gout_ref, wqkv_ref, bqkv_ref, wo_ref, bo_ref,
                 o_ref, kprev_ref, vprev_ref, att_ref):
    rows, qd = att_ref.shape
    blk = ATTN_BLOCK
    pair = 2 * HEAD_DIM
    pairs_per_kv = Q_PER_KV // 2
    first_tile = pl.program_id(1) == 0

    @pl.when(first_tile)
    def _():
        kprev_ref[...] = jnp.zeros_like(kprev_ref)
        vprev_ref[...] = jnp.zeros_like(vprev_ref)

    x = x_ref[...]
    xn = _rms(x, gin_ref[...]).astype(BF16)
    qkv = _dot(xn, wqkv_ref[...]) + bqkv_ref[...]
    q = qkv[:, :qd].astype(BF16)
    k_new = qkv[:, qd:qd + pair] * (1.0 / math.sqrt(HEAD_DIM))
    v_new = qkv[:, qd + pair:qd + 2 * pair]
    k_all = jnp.concatenate([kprev_ref[...], k_new], axis=0)
    v_all = jnp.concatenate([vprev_ref[...], v_new], axis=0)
    kprev_ref[...] = k_new[rows - blk:, :]
    vprev_ref[...] = v_new[rows - blk:, :]

    low = lax.broadcasted_iota(jnp.int32, k_all.shape, 1) < HEAD_DIM

    def placed(t):
        swapped = pltpu.roll(t, HEAD_DIM, axis=1)
        lo = [jnp.where(low, t, 0.0), jnp.where(low, swapped, 0.0)]
        hi = [jnp.where(low, 0.0, swapped), jnp.where(low, 0.0, t)]
        return ([a.astype(BF16) for a in lo], [a.astype(BF16) for a in hi])

    k_lo, k_hi = placed(k_all)
    v_lo, v_hi = placed(v_all)

    qi = lax.broadcasted_iota(jnp.int32, (blk, blk), 0)
    kj = lax.broadcasted_iota(jnp.int32, (blk, blk), 1)
    from_prev = kj > qi
    no_prev = jnp.logical_and(from_prev, first_tile)

    contract_last = (((1,), (1,)), ((), ()))
    for j in range(rows // blk):
        r0 = j * blk
        for kh in range(N_KV_HEADS):
            c0 = kh * pairs_per_kv * pair
            qs = jnp.concatenate(
                [q[r0:r0 + blk, c0 + p * pair:c0 + (p + 1) * pair]
                 for p in range(pairs_per_kv)], axis=0)
            probs = []
            for parity, k_placed in enumerate((k_lo[kh], k_hi[kh])):
                s = lax.dot_general(qs, k_placed[r0:r0 + 2 * blk, :], contract_last,
                                    preferred_element_type=F32)
                parts = []
                for p in range(pairs_per_kv):
                    sp = s[p * blk:(p + 1) * blk, :]
                    comb = jnp.where(from_prev, sp[:, :blk], sp[:, blk:])
                    if j == 0:
                        comb = jnp.where(no_prev, NEG_INF, comb)
                    sink = sink_ref[kh * Q_PER_KV + 2 * p + parity]
                    m = jnp.maximum(jnp.max(comb, axis=-1, keepdims=True), sink)
                    e = jnp.exp(comb - m)
                    den = jnp.sum(e, axis=-1, keepdims=True) + jnp.exp(sink - m)
                    pr = e * (1.0 / den)
                    parts.append(jnp.concatenate(
                        [jnp.where(from_prev, pr, 0.0), jnp.where(from_prev, 0.0, pr)],
                        axis=1).astype(BF16))
                probs.append(jnp.concatenate(parts, axis=0))
            lhs = jnp.concatenate(probs, axis=1)
            rhs = jnp.concatenate([v_lo[kh][r0:r0 + 2 * blk, :],
                                   v_hi[kh][r0:r0 + 2 * blk, :]], axis=0)
            o = _dot(lhs, rhs)
            for p in range(pairs_per_kv):
                att_ref[r0:r0 + blk, c0 + p * pair:c0 + (p + 1) * pair] = (
                    o[p * blk:(p + 1) * blk, :])

    m = _dot(att_ref[...].astype(BF16), wo_ref[...]) + bo_ref[...]
    o_ref[...] = x + _rms(m, gout_ref[...])


def _attn_block(h, batch, g_in, g_out, w_qkv, b_qkv, sinks, w_o, b_o):
    n, d = h.shape
    qd = w_o.shape[0]
    qkv_w = w_qkv.shape[1]
    tiles = n // batch // MIX_ROWS
    assert n == batch * tiles * MIX_ROWS and MIX_ROWS % ATTN_BLOCK == 0
    assert N_KV_HEADS * HEAD_DIM == V7X_LANES
    row_spec = pl.BlockSpec((MIX_ROWS, d), lambda b, s: (b * tiles + s, 0))
    return pl.pallas_call(
        _attn_kernel,
        name="swa",
        grid=(batch, tiles),
        in_specs=[
            pl.BlockSpec(memory_space=pltpu.SMEM),
            row_spec,
            _const_spec((1, d)),
            _const_spec((1, d)),
            _const_spec((d, qkv_w)),
            _const_spec((1, qkv_w)),
            _const_spec((qd, d)),
            _const_spec((1, d)),
        ],
        out_specs=row_spec,
        out_shape=jax.ShapeDtypeStruct((n, d), F32),
        scratch_shapes=[
            pltpu.VMEM((ATTN_BLOCK, V7X_LANES), F32),
            pltpu.VMEM((ATTN_BLOCK, V7X_LANES), F32),
            pltpu.VMEM((MIX_ROWS, qd), F32),
        ],
        compiler_params=pltpu.CompilerParams(
            dimension_semantics=("arbitrary", "arbitrary"),
            vmem_limit_bytes=V7X_VMEM_LIMIT_BYTES),
    )(sinks, h, g_in.reshape(1, d), g_out.reshape(1, d), w_qkv.astype(BF16),
      b_qkv.reshape(1, qkv_w), w_o.astype(BF16), b_o.reshape(1, d))


def kernel(x, norm_ffn, norm_mix, ffn_w_gate, ffn_w_up, ffn_w_down, lru_w_in, lru_b_in, lru_conv_w, lru_conv_b, lru_w_a, lru_b_a, lru_w_i, lru_b_i, lru_lambda, lru_w_out, lru_b_out, attn_w_qkv, attn_b_qkv, attn_sinks, attn_w_o, attn_b_o):
    batch, seq, d = x.shape
    depth = norm_ffn.shape[0]
    h = x.reshape(batch * seq, d)
    for layer in range(depth):
        h = _ffn_block(h, norm_ffn[layer, 0, 0], norm_ffn[layer, 0, 1],
                       ffn_w_gate[layer, 0], ffn_w_up[layer, 0], ffn_w_down[layer, 0])
        j = layer // 2
        if layer % 2 == 0:
            h = _lru_block(h, batch, norm_mix[layer, 0], norm_mix[layer, 1],
                           lru_w_in[j], lru_b_in[j], lru_conv_w[j], lru_conv_b[j],
                           lru_w_a[j], lru_b_a[j], lru_w_i[j], lru_b_i[j],
                           lru_lambda[j], lru_w_out[j], lru_b_out[j])
        else:
            h = _attn_block(h, batch, norm_mix[layer, 0], norm_mix[layer, 1],
                            attn_w_qkv[j], attn_b_qkv[j], attn_sinks[j],
                            attn_w_o[j], attn_b_o[j])
        h = _ffn_block(h, norm_ffn[layer, 1, 0], norm_ffn[layer, 1, 1],
                       ffn_w_gate[layer, 1], ffn_w_up[layer, 1], ffn_w_down[layer, 1])
    return h.reshape(batch, seq, d)
```

```python
import functools
import math

import jax
import jax.numpy as jnp
from jax import lax
from jax.experimental import pallas as pl
from jax.experimental.pallas import tpu as pltpu

F32 = jnp.float32
BF16 = jnp.bfloat16

RMS_EPS = 1e-6
LOG2_E = math.log2(math.e)
NEG_INF = -1e30
LRU_C = 8.0
LRU_HEADS = 4
CONV_WIDTH = 4
HEAD_DIM = 64
N_KV_HEADS = 2
Q_PER_KV = 8
ATTN_BLOCK = 128

V7X_SUBLANES = 8
V7X_LANES = 128
V7X_MXU_DIM = 256
V7X_VMEM_LIMIT_BYTES = 56 * 1024 * 1024

FFN_ROWS = 512
FFN_CHUNK = V7X_MXU_DIM
MIX_ROWS = 512


def _rms(x, g):
    ms = jnp.mean(x * x, axis=-1, keepdims=True)
    return x * lax.rsqrt(ms + RMS_EPS) * g


def _dot(a, b):
    return jnp.dot(a, b, preferred_element_type=F32)


def _sigmoid(v):
    return 1.0 / (1.0 + jnp.exp2(-LOG2_E * v))


def _gelu_tanh(v):
    k1 = -2.0 * math.sqrt(2.0 / math.pi) * LOG2_E
    t = v * (k1 + (k1 * 0.044715) * (v * v))
    return v / (1.0 + jnp.exp2(t))


def _const_spec(shape):
    nd = len(shape)
    return pl.BlockSpec(shape, lambda *_: (0,) * nd)


def _ffn_kernel(x_ref, gin_ref, gout_ref, wg_ref, wu_ref, wd_ref, o_ref, acc_ref):
    x = x_ref[...]
    xn = _rms(x, gin_ref[...]).astype(BF16)
    d_ff = wg_ref.shape[1]
    for c in range(d_ff // FFN_CHUNK):
        sl = slice(c * FFN_CHUNK, (c + 1) * FFN_CHUNK)
        g = _dot(xn, wg_ref[:, sl])
        u = _dot(xn, wu_ref[:, sl])
        mid = (g * jax.nn.sigmoid(g) * u).astype(BF16)
        d = _dot(mid, wd_ref[sl, :])
        if c == 0:
            acc_ref[...] = d
        else:
            acc_ref[...] += d
    o_ref[...] = x + 0.5 * _rms(acc_ref[...], gout_ref[...])


def _ffn_block(h, g_in, g_out, w_gate, w_up, w_down):
    n, d = h.shape
    d_ff = w_gate.shape[1]
    assert n % FFN_ROWS == 0 and d_ff % FFN_CHUNK == 0
    row_spec = pl.BlockSpec((FFN_ROWS, d), lambda i: (i, 0))
    resident = functools.partial(pl.BlockSpec, pipeline_mode=pl.Buffered(1))
    return pl.pallas_call(
        _ffn_kernel,
        name="ffn",
        grid=(n // FFN_ROWS,),
        in_specs=[
            row_spec,
            _const_spec((1, d)),
            _const_spec((1, d)),
            resident((d, d_ff), lambda i: (0, 0)),
            resident((d, d_ff), lambda i: (0, 0)),
            resident((d_ff, d), lambda i: (0, 0)),
        ],
        out_specs=row_spec,
        out_shape=jax.ShapeDtypeStruct((n, d), F32),
        scratch_shapes=[pltpu.VMEM((FFN_ROWS, d), F32)],
        compiler_params=pltpu.CompilerParams(
            dimension_semantics=("arbitrary",),
            vmem_limit_bytes=V7X_VMEM_LIMIT_BYTES),
    )(h, g_in.reshape(1, d), g_out.reshape(1, d),
      w_gate.astype(BF16), w_up.astype(BF16), w_down.astype(BF16))


def _scan_step(a, b, shift, idx):
    keep = idx >= shift
    a_prev = pltpu.roll(a, shift, axis=0)
    b_prev = pltpu.roll(b, shift, axis=0)
    b = jnp.where(keep, a * b_prev, 0.0) + b
    a = jnp.where(keep, a * a_prev, a)
    return a, b


def _lru_kernel(x_ref, gin_ref, gout_ref, win_ref, bin_ref, cw_ref, cb_ref, wai_ref,
                bai_ref, lam_ref, wout_ref, bout_ref, o_ref,
                xcp_ref, hs_ref, tail_ref, hcar_ref, a_ref, b_ref):
    strips, rows, lanes = a_ref.shape
    nseg = V7X_SUBLANES
    seg = rows // nseg
    pitch = xcp_ref.shape[1] // nseg
    width = strips * lanes
    blk = width // LRU_HEADS
    taps = CONV_WIDTH - 1

    @pl.when(pl.program_id(1) == 0)
    def _():
        tail_ref[...] = jnp.zeros_like(tail_ref)
        hcar_ref[...] = jnp.zeros_like(hcar_ref)

    x = x_ref[...]
    xn = _rms(x, gin_ref[...]).astype(BF16)
    proj = _dot(xn, win_ref[...]) + bin_ref[...]
    y = _gelu_tanh(proj[:, :width])

    for c in range(strips):
        for s in range(nseg):
            xcp_ref[c, s * pitch:s * pitch + seg, :] = (
                proj[s * seg:(s + 1) * seg, width + c * lanes:width + (c + 1) * lanes])

    sub = lax.broadcasted_iota(jnp.int32, (nseg, lanes), 0)
    xb_strips = []
    for c in range(strips):
        ls = slice(c * lanes, (c + 1) * lanes)
        regs = [xcp_ref[c, pl.ds(j, nseg, stride=pitch), :] for j in range(seg)]
        head = []
        for k in range(taps):
            last = regs[seg - taps + k]
            prev_tile = tail_ref[c, k * nseg:(k + 1) * nseg, :]
            head.append(pltpu.roll(jnp.where(sub == nseg - 1, prev_tile, last), 1, axis=0))
            tail_ref[c, k * nseg:(k + 1) * nseg, :] = last
        ext = head + regs
        w = [cw_ref[k:k + 1, ls] for k in range(CONV_WIDTH)]
        cb = cb_ref[:, ls]
        out = []
        for j in range(seg):
            acc = cb + w[taps] * ext[j + taps]
            for k in range(taps):
                acc = acc + w[k] * ext[j + k]
            out.append(acc)
        xb_strips.append(jnp.concatenate(out, axis=0))
    xb = jnp.concatenate(xb_strips, axis=1)

    lam = lam_ref[...]
    log_sig_lam = jnp.minimum(lam, 0.0) - jnp.log(1.0 + jnp.exp(-jnp.abs(lam)))
    log2_a_scale = (LRU_C * LOG2_E) * log_sig_lam
    for h in range(LRU_HEADS):
        cs = slice(h * blk, (h + 1) * blk)
        xbh = xb[:, cs]
        gates = _sigmoid(_dot(xbh.astype(BF16), wai_ref[h]) + bai_ref[h])
        a = jnp.exp2(gates[:, :blk] * log2_a_scale[:, cs])
        z = 1.0 - a * a
        mult = jnp.where(z > 0.0, z * lax.rsqrt(z), 0.0)
        b = mult * (gates[:, blk:] * xbh)
        for t in range(blk // lanes):
            cc = h * (blk // lanes) + t
            a_ref[cc] = a[:, t * lanes:(t + 1) * lanes]
            b_ref[cc] = b[:, t * lanes:(t + 1) * lanes]

    for c in range(strips):
        ls = slice(c * lanes, (c + 1) * lanes)
        hz = []
        az = []
        for j in range(seg):
            a_j = a_ref[c, j * nseg:(j + 1) * nseg, :]
            b_j = b_ref[c, j * nseg:(j + 1) * nseg, :]
            hz.append(b_j if j == 0 else a_j * hz[-1] + b_j)
            az.append(a_j if j == 0 else a_j * az[-1])
        a_tot, h_tot = az[-1], hz[-1]
        shift = 1
        while shift < nseg:
            a_tot, h_tot = _scan_step(a_tot, h_tot, shift, sub)
            shift *= 2
        h_in = hcar_ref[:, ls]
        seg_out = a_tot * h_in + h_tot
        hcar_ref[:, ls] = seg_out[nseg - 1:nseg, :]
        seg_in = jnp.where(sub >= 1, pltpu.roll(seg_out, 1, axis=0), h_in)
        for j in range(seg):
            hs_ref[c, pl.ds(j, nseg, stride=pitch), :] = hz[j] + az[j] * seg_in

    hseq = jnp.concatenate(
        [jnp.concatenate([hs_ref[c, s * pitch:s * pitch + seg, :] for s in range(nseg)],
                         axis=0) for c in range(strips)], axis=1)
    m = _dot((hseq * y).astype(BF16), wout_ref[...]) + bout_ref[...]
    o_ref[...] = x + _rms(m, gout_ref[...])


def _lru_block(h, batch, g_in, g_out, w_in, b_in, conv_w, conv_b, w_a, b_a, w_i, b_i,
               lam, w_out, b_out):
    n, d = h.shape
    width = w_out.shape[0]
    blk = width // LRU_HEADS
    tiles = n // batch // MIX_ROWS
    assert n == batch * tiles * MIX_ROWS
    strips = width // V7X_LANES
    seg_tiles = MIX_ROWS // V7X_SUBLANES // V7X_SUBLANES
    padded_rows = V7X_SUBLANES * V7X_SUBLANES * (seg_tiles + 1 - seg_tiles % 2)
    w_ai = jnp.concatenate([w_a, w_i], axis=-1).astype(BF16)
    b_ai = jnp.concatenate([b_a.reshape(LRU_HEADS, 1, blk),
                            b_i.reshape(LRU_HEADS, 1, blk)], axis=-1)
    row_spec = pl.BlockSpec((MIX_ROWS, d), lambda b, s: (b * tiles + s, 0))
    return pl.pallas_call(
        _lru_kernel,
        name="rglru",
        grid=(batch, tiles),
        in_specs=[
            row_spec,
            _const_spec((1, d)),
            _const_spec((1, d)),
            _const_spec((d, 2 * width)),
            _const_spec((1, 2 * width)),
            _const_spec((CONV_WIDTH, width)),
            _const_spec((1, width)),
            _const_spec((LRU_HEADS, blk, 2 * blk)),
            _const_spec((LRU_HEADS, 1, 2 * blk)),
            _const_spec((1, width)),
            _const_spec((width, d)),
            _const_spec((1, d)),
        ],
        out_specs=row_spec,
        out_shape=jax.ShapeDtypeStruct((n, d), F32),
        scratch_shapes=[
            pltpu.VMEM((strips, padded_rows, V7X_LANES), F32),
            pltpu.VMEM((strips, padded_rows, V7X_LANES), F32),
            pltpu.VMEM((strips, (CONV_WIDTH - 1) * V7X_SUBLANES, V7X_LANES), F32),
            pltpu.VMEM((1, width), F32),
            pltpu.VMEM((strips, MIX_ROWS, V7X_LANES), F32),
            pltpu.VMEM((strips, MIX_ROWS, V7X_LANES), F32),
        ],
        compiler_params=pltpu.CompilerParams(
            dimension_semantics=("arbitrary", "arbitrary"),
            vmem_limit_bytes=V7X_VMEM_LIMIT_BYTES),
    )(h, g_in.reshape(1, d), g_out.reshape(1, d), w_in.astype(BF16),
      b_in.reshape(1, -1), conv_w, conv_b.reshape(1, -1), w_ai, b_ai,
      lam.reshape(1, -1), w_out.astype(BF16), b_out.reshape(1, d))


def _attn_kernel(sink_ref, x_ref, gin_ref, gout_ref, wqkv_ref, bqkv_ref, wo_ref, bo_ref,
                 o_ref, kprev_ref, vprev_ref, att_ref):
    rows, qd = att_ref.shape
    blk = ATTN_BLOCK
    pair = 2 * HEAD_DIM
    pairs_per_kv = Q_PER_KV // 2
    first_tile = pl.program_id(1) == 0

    @pl.when(first_tile)
    def _():
        kprev_ref[...] = jnp.zeros_like(kprev_ref)
        vprev_ref[...] = jnp.zeros_like(vprev_ref)

    x = x_ref[...]
    xn = _rms(x, gin_ref[...]).astype(BF16)
    qkv = _dot(xn, wqkv_ref[...]) + bqkv_ref[...]
    q = qkv[:, :qd].astype(BF16)
    k_new = qkv[:, qd:qd + pair] * (1.0 / math.sqrt(HEAD_DIM))
    v_new = qkv[:, qd + pair:qd + 2 * pair]
    k_all = jnp.concatenate([kprev_ref[...], k_new], axis=0)
    v_all = jnp.concatenate([vprev_ref[...], v_new], axis=0)
    kprev_ref[...] = k_new[rows - blk:, :]
    vprev_ref[...] = v_new[rows - blk:, :]

    low = lax.broadcasted_iota(jnp.int32, k_all.shape, 1) < HEAD_DIM

    def placed(t):
        swapped = pltpu.roll(t, HEAD_DIM, axis=1)
        lo = [jnp.where(low, t, 0.0), jnp.where(low, swapped, 0.0)]
        hi = [jnp.where(low, 0.0, swapped), jnp.where(low, 0.0, t)]
        return ([a.astype(BF16) for a in lo], [a.astype(BF16) for a in hi])

    k_lo, k_hi = placed(k_all)
    v_lo, v_hi = placed(v_all)

    qi = lax.broadcasted_iota(jnp.int32, (blk, blk), 0)
    kj = lax.broadcasted_iota(jnp.int32, (blk, blk), 1)
    from_prev = kj > qi
    no_prev = jnp.logical_and(from_prev, first_tile)

    contract_last = (((1,), (1,)), ((), ()))
    for j in range(rows // blk):
        r0 = j * blk
        for kh in range(N_KV_HEADS):
            c0 = kh * pairs_per_kv * pair
            qs = jnp.concatenate(
                [q[r0:r0 + blk, c0 + p * pair:c0 + (p + 1) * pair]
                 for p in range(pairs_per_kv)], axis=0)
            probs = []
            for parity, k_placed in enumerate((k_lo[kh], k_hi[kh])):
                s = lax.dot_general(qs, k_placed[r0:r0 + 2 * blk, :], contract_last,
                                    preferred_element_type=F32)
                parts = []
                for p in range(pairs_per_kv):
                    sp = s[p * blk:(p + 1) * blk, :]
                    comb = jnp.where(from_prev, sp[:, :blk], sp[:, blk:])
                    if j == 0:
                        comb = jnp.where(no_prev, NEG_INF, comb)
                    sink = sink_ref[kh * Q_PER_KV + 2 * p + parity]
                    m = jnp.maximum(jnp.max(comb, axis=-1, keepdims=True), sink)
                    e = jnp.exp(comb - m)
                    den = jnp.sum(e, axis=-1, keepdims=True) + jnp.exp(sink - m)
                    pr = e * (1.0 / den)
                    parts.append(jnp.concatenate(
                        [jnp.where(from_prev, pr, 0.0), jnp.where(from_prev, 0.0, pr)],
                        axis=1).astype(BF16))
                probs.append(jnp.concatenate(parts, axis=0))
            lhs = jnp.concatenate(probs, axis=1)
            rhs = jnp.concatenate([v_lo[kh][r0:r0 + 2 * blk, :],
                                   v_hi[kh][r0:r0 + 2 * blk, :]], axis=0)
            o = _dot(lhs, rhs)
            for p in range(pairs_per_kv):
                att_ref[r0:r0 + blk, c0 + p * pair:c0 + (p + 1) * pair] = (
                    o[p * blk:(p + 1) * blk, :])

    m = _dot(att_ref[...].astype(BF16), wo_ref[...]) + bo_ref[...]
    o_ref[...] = x + _rms(m, gout_ref[...])


def _attn_block(h, batch, g_in, g_out, w_qkv, b_qkv, sinks, w_o, b_o):
    n, d = h.shape
    qd = w_o.shape[0]
    qkv_w = w_qkv.shape[1]
    tiles = n // batch // MIX_ROWS
    assert n == batch * tiles * MIX_ROWS and MIX_ROWS % ATTN_BLOCK == 0
    assert N_KV_HEADS * HEAD_DIM == V7X_LANES
    row_spec = pl.BlockSpec((MIX_ROWS, d), lambda b, s: (b * tiles + s, 0))
    return pl.pallas_call(
        _attn_kernel,
        name="swa",
        grid=(batch, tiles),
        in_specs=[
            pl.BlockSpec(memory_space=pltpu.SMEM),
            row_spec,
            _const_spec((1, d)),
            _const_spec((1, d)),
            _const_spec((d, qkv_w)),
            _const_spec((1, qkv_w)),
            _const_spec((qd, d)),
            _const_spec((1, d)),
        ],
        out_specs=row_spec,
        out_shape=jax.ShapeDtypeStruct((n, d), F32),
        scratch_shapes=[
            pltpu.VMEM((ATTN_BLOCK, V7X_LANES), F32),
            pltpu.VMEM((ATTN_BLOCK, V7X_LANES), F32),
            pltpu.VMEM((MIX_ROWS, qd), F32),
        ],
        compiler_params=pltpu.CompilerParams(
            dimension_semantics=("arbitrary", "arbitrary"),
            vmem_limit_bytes=V7X_VMEM_LIMIT_BYTES),
    )(sinks, h, g_in.reshape(1, d), g_out.reshape(1, d), w_qkv.astype(BF16),
      b_qkv.reshape(1, qkv_w), w_o.astype(BF16), b_o.reshape(1, d))


def kernel(x, norm_ffn, norm_mix, ffn_w_gate, ffn_w_up, ffn_w_down, lru_w_in, lru_b_in, lru_conv_w, lru_conv_b, lru_w_a, lru_b_a, lru_w_i, lru_b_i, lru_lambda, lru_w_out, lru_b_out, attn_w_qkv, attn_b_qkv, attn_sinks, attn_w_o, attn_b_o):
    batch, seq, d = x.shape
    depth = norm_ffn.shape[0]
    h = x.reshape(batch * seq, d)
    for layer in range(depth):
        h = _ffn_block(h, norm_ffn[layer, 0, 0], norm_ffn[layer, 0, 1],
                       ffn_w_gate[layer, 0], ffn_w_up[layer, 0], ffn_w_down[layer, 0])
        j = layer // 2
        if layer % 2 == 0:
            h = _lru_block(h, batch, norm_mix[layer, 0], norm_mix[layer, 1],
                           lru_w_in[j], lru_b_in[j], lru_conv_w[j], lru_conv_b[j],
                           lru_w_a[j], lru_b_a[j], lru_w_i[j], lru_b_i[j],
                           lru_lambda[j], lru_w_out[j], lru_b_out[j])
        else:
            h = _attn_block(h, batch, norm_mix[layer, 0], norm_mix[layer, 1],
                            attn_w_qkv[j], attn_b_qkv[j], attn_sinks[j],
                            attn_w_o[j], attn_b_o[j])
        h = _ffn_block(h, norm_ffn[layer, 1, 0], norm_ffn[layer, 1, 1],
                       ffn_w_gate[layer, 1], ffn_w_up[layer, 1], ffn_w_down[layer, 1])
    return h.reshape(batch, seq, d)
```

```python
import functools
import math

import jax
import jax.numpy as jnp
from jax import lax
from jax.experimental import pallas as pl
from jax.experimental.pallas import tpu as pltpu

F32 = jnp.float32
BF16 = jnp.bfloat16

RMS_EPS = 1e-6
LOG2_E = math.log2(math.e)
NEG_INF = -1e30
LRU_C = 8.0
LRU_HEADS = 4
CONV_WIDTH = 4
HEAD_DIM = 64
N_KV_HEADS = 2
Q_PER_KV = 8
ATTN_BLOCK = 128

V7X_SUBLANES = 8
V7X_LANES = 128
V7X_MXU_DIM = 256
V7X_VMEM_LIMIT_BYTES = 56 * 1024 * 1024

FFN_ROWS = 512
FFN_TILES_PER_STEP = 2
FFN_CHUNK = V7X_MXU_DIM
MIX_ROWS = 512
MIX_SEQS = 2
EPILOGUE_PIECES = 4
FFN_SKEW = 10
LRU_SKEW = 9
ATTN_SKEW = 7


def _rms(x, g):
    ms = jnp.mean(x * x, axis=-1, keepdims=True)
    return x * lax.rsqrt(ms + RMS_EPS) * g


def _dot(a, b):
    return jnp.dot(a, b, preferred_element_type=F32)


def _sigmoid(v):
    return 1.0 / (1.0 + jnp.exp2(-LOG2_E * v))


def _gelu_tanh(v):
    k1 = -2.0 * math.sqrt(2.0 / math.pi) * LOG2_E
    t = v * (k1 + (k1 * 0.044715) * (v * v))
    return v / (1.0 + jnp.exp2(t))


def _const_spec(shape):
    nd = len(shape)
    return pl.BlockSpec(shape, lambda *_: (0,) * nd)


def _resident_spec(shape):
    nd = len(shape)
    return pl.BlockSpec(shape, lambda *_: (0,) * nd, pipeline_mode=pl.Buffered(1))


def _run_interleaved(chains, skew):
    chains = list(chains)
    done = [False] * len(chains)
    rnd = 0
    while not all(done):
        for i, chain in enumerate(chains):
            if rnd >= i * skew and not done[i]:
                try:
                    next(chain)
                except StopIteration:
                    done[i] = True
        rnd += 1


def _epilogue(x_ref, m_ref, gout_ref, o_ref, scale):
    rows = x_ref.shape[0]
    piece = rows // EPILOGUE_PIECES
    for r in range(EPILOGUE_PIECES):
        rs = slice(r * piece, (r + 1) * piece)
        o_ref[rs, :] = x_ref[rs, :] + scale * _rms(m_ref[rs, :], gout_ref[...])
        yield


def _ffn_chain(x_ref, gin_ref, gout_ref, wg_ref, wu_ref, wd_ref, o_ref, acc_ref):
    d_ff = wg_ref.shape[1]
    xn = _rms(x_ref[...], gin_ref[...]).astype(BF16)
    yield
    for c, lo in enumerate(range(0, d_ff, FFN_CHUNK)):
        sl = slice(lo, min(lo + FFN_CHUNK, d_ff))
        g = _dot(xn, wg_ref[:, sl])
        u = _dot(xn, wu_ref[:, sl])
        mid = (g * jax.nn.sigmoid(g) * u).astype(BF16)
        d = _dot(mid, wd_ref[sl, :])
        if c == 0:
            acc_ref[...] = d
        else:
            acc_ref[...] += d
        yield
    yield from _epilogue(x_ref, acc_ref, gout_ref, o_ref, 0.5)


def _ffn_kernel(x_ref, gin_ref, gout_ref, wg_ref, wu_ref, wd_ref, o_ref, acc_ref):
    chains = []
    for t in range(FFN_TILES_PER_STEP):
        rs = pl.ds(t * FFN_ROWS, FFN_ROWS)
        chains.append(_ffn_chain(x_ref.at[rs, :], gin_ref, gout_ref, wg_ref, wu_ref, wd_ref,
                                 o_ref.at[rs, :], acc_ref.at[t]))
    _run_interleaved(chains, FFN_SKEW)


def _ffn_block(h, g_in, g_out, w_gate, w_up, w_down, layer, which):
    n, d = h.shape
    d_ff = w_gate.shape[-1]
    step_rows = FFN_TILES_PER_STEP * FFN_ROWS
    assert n % step_rows == 0 and d_ff % V7X_MXU_DIM == 0
    row_spec = pl.BlockSpec((step_rows, d), lambda i: (i, 0))

    def weight_spec(rows, cols):
        return pl.BlockSpec((None, None, rows, cols), lambda i: (layer, which, 0, 0),
                            pipeline_mode=pl.Buffered(1))
    return pl.pallas_call(
        _ffn_kernel,
        name="ffn",
        grid=(n // step_rows,),
        in_specs=[
            row_spec,
            _const_spec((1, d)),
            _const_spec((1, d)),
            weight_spec(d, d_ff),
            weight_spec(d, d_ff),
            weight_spec(d_ff, d),
        ],
        out_specs=row_spec,
        out_shape=jax.ShapeDtypeStruct((n, d), F32),
        scratch_shapes=[pltpu.VMEM((FFN_TILES_PER_STEP, FFN_ROWS, d), F32)],
        compiler_params=pltpu.CompilerParams(
            dimension_semantics=("arbitrary",),
            vmem_limit_bytes=V7X_VMEM_LIMIT_BYTES),
    )(h, g_in.reshape(1, d), g_out.reshape(1, d), w_gate, w_up, w_down)


def _scan_step(a, b, shift, idx):
    keep = idx >= shift
    a_prev = pltpu.roll(a, shift, axis=0)
    b_prev = pltpu.roll(b, shift, axis=0)
    b = jnp.where(keep, a * b_prev, 0.0) + b
    a = jnp.where(keep, a * a_prev, a)
    return a, b


def _lru_chain(x_ref, gin_ref, gout_ref, win_ref, bin_ref, cw_ref, cb_ref, wai_ref,
               bai_ref, lam_ref, wout_ref, bout_ref, o_ref,
               xcp_ref, tail_ref, hcar_ref, a_ref, b_ref, y_ref, m_ref):
    strips, rows, lanes = a_ref.shape
    nseg = V7X_SUBLANES
    seg = rows // nseg
    pitch = xcp_ref.shape[1] // nseg
    width = strips * lanes
    blk = width // LRU_HEADS
    head_strips = blk // lanes
    taps = CONV_WIDTH - 1

    xn = _rms(x_ref[...], gin_ref[...]).astype(BF16)
    yield

    sub = lax.broadcasted_iota(jnp.int32, (nseg, lanes), 0)
    lam = lam_ref[...]
    log_sig_lam = jnp.minimum(lam, 0.0) - jnp.log(1.0 + jnp.exp(-jnp.abs(lam)))
    log2_a_scale = (LRU_C * LOG2_E) * log_sig_lam
    hy_heads = {}

    def project(h):
        cs = slice(h * blk, (h + 1) * blk)
        xs = slice(width + h * blk, width + (h + 1) * blk)
        y_ref[:, cs] = _gelu_tanh(_dot(xn, win_ref[:, cs]) + bin_ref[:, cs])
        xc = _dot(xn, win_ref[:, xs]) + bin_ref[:, xs]
        for t in range(head_strips):
            for s in range(nseg):
                xcp_ref[h * head_strips + t, s * pitch:s * pitch + seg, :] = (
                    xc[s * seg:(s + 1) * seg, t * lanes:(t + 1) * lanes])

    def conv(h):
        for c in range(h * head_strips, (h + 1) * head_strips):
            ls = slice(c * lanes, (c + 1) * lanes)
            window = []
            for k in range(taps):
                last = xcp_ref[c, pl.ds(seg - taps + k, nseg, stride=pitch), :]
                prev_tile = tail_ref[c, k * nseg:(k + 1) * nseg, :]
                window.append(
                    pltpu.roll(jnp.where(sub == nseg - 1, prev_tile, last), 1, axis=0))
                tail_ref[c, k * nseg:(k + 1) * nseg, :] = last
            w = [cw_ref[k:k + 1, ls] for k in range(CONV_WIDTH)]
            cb = cb_ref[:, ls]
            for j in range(seg):
                cur = xcp_ref[c, pl.ds(j, nseg, stride=pitch), :]
                acc = cb + w[taps] * cur
                for k in range(taps):
                    acc = acc + w[k] * window[k]
                b_ref[c, j * nseg:(j + 1) * nseg, :] = acc
                window = window[1:] + [cur]

    def gates(h):
        cs = slice(h * blk, (h + 1) * blk)
        xbh = jnp.concatenate(
            [b_ref[c] for c in range(h * head_strips, (h + 1) * head_strips)], axis=1)
        gate = _sigmoid(_dot(xbh.astype(BF16), wai_ref[h]) + bai_ref[h])
        a = jnp.exp2(gate[:, :blk] * log2_a_scale[:, cs])
        z = 1.0 - a * a
        mult = jnp.where(z > 0.0, z * lax.rsqrt(z), 0.0)
        b = mult * (gate[:, blk:] * xbh)
        for t in range(head_strips):
            a_ref[h * head_strips + t] = a[:, t * lanes:(t + 1) * lanes]
            b_ref[h * head_strips + t] = b[:, t * lanes:(t + 1) * lanes]

    def scan(h):
        for c in range(h * head_strips, (h + 1) * head_strips):
            ls = slice(c * lanes, (c + 1) * lanes)
            h_tot = b_ref[c, 0:nseg, :]
            a_tot = a_ref[c, 0:nseg, :]
            for j in range(1, seg):
                a_j = a_ref[c, j * nseg:(j + 1) * nseg, :]
                h_tot = a_j * h_tot + b_ref[c, j * nseg:(j + 1) * nseg, :]
                a_tot = a_j * a_tot
            shift = 1
            while shift < nseg:
                a_tot, h_tot = _scan_step(a_tot, h_tot, shift, sub)
                shift *= 2
            h_in = hcar_ref[:, ls]
            seg_out = a_tot * h_in + h_tot
            hcar_ref[:, ls] = seg_out[nseg - 1:nseg, :]
            state = jnp.where(sub >= 1, pltpu.roll(seg_out, 1, axis=0), h_in)
            for j in range(seg):
                state = (a_ref[c, j * nseg:(j + 1) * nseg, :] * state
                         + b_ref[c, j * nseg:(j + 1) * nseg, :])
                xcp_ref[c, pl.ds(j, nseg, stride=pitch), :] = state
        hseq = jnp.concatenate(
            [jnp.concatenate([xcp_ref[c, s * pitch:s * pitch + seg, :] for s in range(nseg)],
                             axis=0)
             for c in range(h * head_strips, (h + 1) * head_strips)], axis=1)
        hy_heads[h] = (hseq * y_ref[:, h * blk:(h + 1) * blk]).astype(BF16)

    def out_project(h):
        part = _dot(hy_heads.pop(h), wout_ref[h * blk:(h + 1) * blk, :])
        if h == 0:
            m_ref[...] = part + bout_ref[...]
        else:
            m_ref[...] += part

    stages = (project, conv, gates, scan, out_project)
    lag = (0, 2, 3, 5, 7)
    step = 4
    slots = {}
    for h in range(LRU_HEADS):
        for stage, d in zip(stages, lag):
            slots.setdefault(h * step + d, []).append((stage, h))
    for slot in sorted(slots):
        for stage, h in slots[slot]:
            stage(h)
        yield

    yield from _epilogue(x_ref, m_ref, gout_ref, o_ref, 1.0)


def _lru_kernel(x_ref, gin_ref, gout_ref, win_ref, bin_ref, cw_ref, cb_ref, wai_ref,
                bai_ref, lam_ref, wout_ref, bout_ref, o_ref, *scratch):
    tail_ref, hcar_ref = scratch[1], scratch[2]

    @pl.when(pl.program_id(1) == 0)
    def _():
        tail_ref[...] = jnp.zeros_like(tail_ref)
        hcar_ref[...] = jnp.zeros_like(hcar_ref)

    _run_interleaved(
        [_lru_chain(x_ref.at[q], gin_ref, gout_ref, win_ref, bin_ref, cw_ref, cb_ref,
                    wai_ref, bai_ref, lam_ref, wout_ref, bout_ref, o_ref.at[q],
                    *[ref.at[q] for ref in scratch]) for q in range(MIX_SEQS)],
        LRU_SKEW)


def _lru_block(h, batch, g_in, g_out, w_in, b_in, conv_w, conv_b, w_a, b_a, w_i, b_i,
               lam, w_out, b_out):
    n, d = h.shape
    width = w_out.shape[0]
    blk = width // LRU_HEADS
    seq = n // batch
    assert n == batch * seq and seq % MIX_ROWS == 0 and batch % MIX_SEQS == 0
    strips = width // V7X_LANES
    seg_tiles = MIX_ROWS // V7X_SUBLANES // V7X_SUBLANES
    padded_rows = V7X_SUBLANES * V7X_SUBLANES * (seg_tiles + 1 - seg_tiles % 2)
    w_ai = jnp.concatenate([w_a, w_i], axis=-1).astype(BF16)
    b_ai = jnp.concatenate([b_a.reshape(LRU_HEADS, 1, blk),
                            b_i.reshape(LRU_HEADS, 1, blk)], axis=-1)
    row_spec = pl.BlockSpec((MIX_SEQS, MIX_ROWS, d), lambda b, s: (b, s, 0))
    return pl.pallas_call(
        _lru_kernel,
        name="rglru",
        grid=(batch // MIX_SEQS, seq // MIX_ROWS),
        in_specs=[
            row_spec,
            _const_spec((1, d)),
            _const_spec((1, d)),
            _resident_spec((d, 2 * width)),
            _const_spec((1, 2 * width)),
            _const_spec((CONV_WIDTH, width)),
            _const_spec((1, width)),
            _resident_spec((LRU_HEADS, blk, 2 * blk)),
            _const_spec((LRU_HEADS, 1, 2 * blk)),
            _const_spec((1, width)),
            _resident_spec((width, d)),
            _const_spec((1, d)),
        ],
        out_specs=row_spec,
        out_shape=jax.ShapeDtypeStruct((batch, seq, d), F32),
        scratch_shapes=[
            pltpu.VMEM((MIX_SEQS, strips, padded_rows, V7X_LANES), F32),
            pltpu.VMEM((MIX_SEQS, strips, (CONV_WIDTH - 1) * V7X_SUBLANES, V7X_LANES), F32),
            pltpu.VMEM((MIX_SEQS, 1, width), F32),
            pltpu.VMEM((MIX_SEQS, strips, MIX_ROWS, V7X_LANES), F32),
            pltpu.VMEM((MIX_SEQS, strips, MIX_ROWS, V7X_LANES), F32),
            pltpu.VMEM((MIX_SEQS, MIX_ROWS, width), F32),
            pltpu.VMEM((MIX_SEQS, MIX_ROWS, d), F32),
        ],
        compiler_params=pltpu.CompilerParams(
            dimension_semantics=("arbitrary", "arbitrary"),
            vmem_limit_bytes=V7X_VMEM_LIMIT_BYTES),
    )(h.reshape(batch, seq, d), g_in.reshape(1, d), g_out.reshape(1, d), w_in.astype(BF16),
      b_in.reshape(1, -1), conv_w, conv_b.reshape(1, -1), w_ai, b_ai,
      lam.reshape(1, -1), w_out.astype(BF16), b_out.reshape(1, d)).reshape(n, d)


def _attn_chain(first_tile, sink_ref, x_ref, gin_ref, gout_ref, wqkv_ref, bqkv_ref, wo_ref,
                bo_ref, o_ref, kprev_ref, vprev_ref, att_ref, m_ref):
    rows, qd = att_ref.shape
    blk = ATTN_BLOCK
    pair = 2 * HEAD_DIM
    pairs_per_kv = Q_PER_KV // 2
    kv_cols = pairs_per_kv * pair
    chunk = V7X_MXU_DIM

    xn = _rms(x_ref[...], gin_ref[...]).astype(BF16)
    yield

    def project(cs):
        return _dot(xn, wqkv_ref[:, cs]) + bqkv_ref[:, cs]

    kv = project(slice(qd, qd + 2 * pair))
    k_new = kv[:, :pair] * (1.0 / math.sqrt(HEAD_DIM))
    v_new = kv[:, pair:]
    k_all = jnp.concatenate([kprev_ref[...], k_new], axis=0)
    v_all = jnp.concatenate([vprev_ref[...], v_new], axis=0)
    kprev_ref[...] = k_new[rows - blk:, :]
    vprev_ref[...] = v_new[rows - blk:, :]

    low = lax.broadcasted_iota(jnp.int32, k_all.shape, 1) < HEAD_DIM

    def placed(t):
        swapped = pltpu.roll(t, HEAD_DIM, axis=1)
        lo = [jnp.where(low, t, 0.0), jnp.where(low, swapped, 0.0)]
        hi = [jnp.where(low, 0.0, swapped), jnp.where(low, 0.0, t)]
        return ([a.astype(BF16) for a in lo], [a.astype(BF16) for a in hi])

    k_lo, k_hi = placed(k_all)
    v_lo, v_hi = placed(v_all)
    yield

    q_chunks = []
    for c in range(qd // chunk):
        q_chunks.append(project(slice(c * chunk, (c + 1) * chunk)).astype(BF16))
        yield

    def q_tile(r0, t):
        per = chunk // pair
        return q_chunks[t // per][r0:r0 + blk, (t % per) * pair:(t % per + 1) * pair]

    qi = lax.broadcasted_iota(jnp.int32, (blk, blk), 0)
    kj = lax.broadcasted_iota(jnp.int32, (blk, blk), 1)
    from_prev = kj > qi
    no_prev = jnp.logical_and(from_prev, first_tile)

    contract_last = (((1,), (1,)), ((), ()))
    for j in range(rows // blk):
        r0 = j * blk
        for kh in range(N_KV_HEADS):
            t0 = kh * pairs_per_kv
            qs = jnp.concatenate([q_tile(r0, t0 + p) for p in range(pairs_per_kv)],
                                 axis=0)
            probs = []
            for parity, k_placed in enumerate((k_lo[kh], k_hi[kh])):
                s = lax.dot_general(qs, k_placed[r0:r0 + 2 * blk, :], contract_last,
                                    preferred_element_type=F32)
                parts = []
                for p in range(pairs_per_kv):
                    sp = s[p * blk:(p + 1) * blk, :]
                    comb = jnp.where(from_prev, sp[:, :blk], sp[:, blk:])
                    if j == 0:
                        comb = jnp.where(no_prev, NEG_INF, comb)
                    sink = sink_ref[kh * Q_PER_KV + 2 * p + parity]
                    m = jnp.maximum(jnp.max(comb, axis=-1, keepdims=True), sink)
                    e = jnp.exp(comb - m)
                    den = jnp.sum(e, axis=-1, keepdims=True) + jnp.exp(sink - m)
                    pr = e * (1.0 / den)
                    parts.append(jnp.concatenate(
                        [jnp.where(from_prev, pr, 0.0), jnp.where(from_prev, 0.0, pr)],
                        axis=1).astype(BF16))
                probs.append(jnp.concatenate(parts, axis=0))
            lhs = jnp.concatenate(probs, axis=1)
            rhs = jnp.concatenate([v_lo[kh][r0:r0 + 2 * blk, :],
                                   v_hi[kh][r0:r0 + 2 * blk, :]], axis=0)
            o = _dot(lhs, rhs)
            for p in range(pairs_per_kv):
                att_ref[r0:r0 + blk, (t0 + p) * pair:(t0 + p + 1) * pair] = (
                    o[p * blk:(p + 1) * blk, :])
            yield

    att = att_ref[...].astype(BF16)
    for c in range(m_ref.shape[1] // chunk):
        cs = slice(c * chunk, (c + 1) * chunk)
        m_ref[:, cs] = _dot(att, wo_ref[:, cs]) + bo_ref[:, cs]
        yield

    yield from _epilogue(x_ref, m_ref, gout_ref, o_ref, 1.0)


def _attn_kernel(sink_ref, x_ref, gin_ref, gout_ref, wqkv_ref, bqkv_ref, wo_ref, bo_ref,
                 o_ref, *scratch):
    kprev_ref, vprev_ref = scratch[0], scratch[1]
    first_tile = pl.program_id(1) == 0

    @pl.when(first_tile)
    def _():
        kprev_ref[...] = jnp.zeros_like(kprev_ref)
        vprev_ref[...] = jnp.zeros_like(vprev_ref)

    _run_interleaved(
        [_attn_chain(first_tile, sink_ref, x_ref.at[q], gin_ref, gout_ref, wqkv_ref, bqkv_ref,
                     wo_ref, bo_ref, o_ref.at[q], *[ref.at[q] for ref in scratch])
         for q in range(MIX_SEQS)],
        ATTN_SKEW)


def _attn_block(h, batch, g_in, g_out, w_qkv, b_qkv, sinks, w_o, b_o):
    n, d = h.shape
    qd = w_o.shape[0]
    qkv_w = w_qkv.shape[1]
    seq = n // batch
    assert n == batch * seq and seq % MIX_ROWS == 0 and batch % MIX_SEQS == 0
    assert MIX_ROWS % ATTN_BLOCK == 0
    assert N_KV_HEADS * HEAD_DIM == V7X_LANES
    row_spec = pl.BlockSpec((MIX_SEQS, MIX_ROWS, d), lambda b, s: (b, s, 0))
    return pl.pallas_call(
        _attn_kernel,
        name="swa",
        grid=(batch // MIX_SEQS, seq // MIX_ROWS),
        in_specs=[
            pl.BlockSpec(memory_space=pltpu.SMEM),
            row_spec,
            _const_spec((1, d)),
            _const_spec((1, d)),
            _resident_spec((d, qkv_w)),
            _const_spec((1, qkv_w)),
            _resident_spec((qd, d)),
            _const_spec((1, d)),
        ],
        out_specs=row_spec,
        out_shape=jax.ShapeDtypeStruct((batch, seq, d), F32),
        scratch_shapes=[
            pltpu.VMEM((MIX_SEQS, ATTN_BLOCK, V7X_LANES), F32),
            pltpu.VMEM((MIX_SEQS, ATTN_BLOCK, V7X_LANES), F32),
            pltpu.VMEM((MIX_SEQS, MIX_ROWS, qd), F32),
            pltpu.VMEM((MIX_SEQS, MIX_ROWS, d), F32),
        ],
        compiler_params=pltpu.CompilerParams(
            dimension_semantics=("arbitrary", "arbitrary"),
            vmem_limit_bytes=V7X_VMEM_LIMIT_BYTES),
    )(sinks, h.reshape(batch, seq, d), g_in.reshape(1, d), g_out.reshape(1, d),
      w_qkv.astype(BF16), b_qkv.reshape(1, qkv_w), w_o.astype(BF16),
      b_o.reshape(1, d)).reshape(n, d)


def kernel(x, norm_ffn, norm_mix, ffn_w_gate, ffn_w_up, ffn_w_down, lru_w_in, lru_b_in, lru_conv_w, lru_conv_b, lru_w_a, lru_b_a, lru_w_i, lru_b_i, lru_lambda, lru_w_out, lru_b_out, attn_w_qkv, attn_b_qkv, attn_sinks, attn_w_o, attn_b_o):
    batch, seq, d = x.shape
    depth = norm_ffn.shape[0]
    h = x.reshape(batch * seq, d)
    w_gate, w_up, w_down = (w.astype(BF16) for w in (ffn_w_gate, ffn_w_up, ffn_w_down))
    for layer in range(depth):
        h = _ffn_block(h, norm_ffn[layer, 0, 0], norm_ffn[layer, 0, 1],
                       w_gate, w_up, w_down, layer, 0)
        j = layer // 2
        if layer % 2 == 0:
            h = _lru_block(h, batch, norm_mix[layer, 0], norm_mix[layer, 1],
                           lru_w_in[j], lru_b_in[j], lru_conv_w[j], lru_conv_b[j],
                           lru_w_a[j], lru_b_a[j], lru_w_i[j], lru_b_i[j],
                           lru_lambda[j], lru_w_out[j], lru_b_out[j])
        else:
            h = _attn_block(h, batch, norm_mix[layer, 0], norm_mix[layer, 1],
                            attn_w_qkv[j], attn_b_qkv[j], attn_sinks[j],
                            attn_w_o[j], attn_b_o[j])
        h = _ffn_block(h, norm_ffn[layer, 1, 0], norm_ffn[layer, 1, 1],
                       w_gate, w_up, w_down, layer, 1)
    return h.reshape(batch, seq, d)
```

```python
import math

import jax
import jax.numpy as jnp
from jax import lax
from jax.experimental import pallas as pl
from jax.experimental.pallas import tpu as pltpu

F32 = jnp.float32
BF16 = jnp.bfloat16

RMS_EPS = 1e-6
LOG2_E = math.log2(math.e)
NEG_INF = -1e30
LRU_C = 8.0
LRU_HEADS = 4
CONV_WIDTH = 4
HEAD_DIM = 64
N_KV_HEADS = 2
Q_PER_KV = 8
ATTN_BLOCK = 128

V7X_SUBLANES = 8
V7X_LANES = 128
V7X_MXU_DIM = 256
V7X_VMEM_LIMIT_BYTES = 56 * 1024 * 1024

FFN_ROWS = 512
FFN_CHUNK = V7X_MXU_DIM
MIX_ROWS = 512
ATTN_BLOCKS_PER_GROUP = 2


def _rms(x, g):
    ms = jnp.mean(x * x, axis=-1, keepdims=True)
    return x * lax.rsqrt(ms + RMS_EPS) * g


def _dot(a, b):
    return jnp.dot(a, b, preferred_element_type=F32)


def _sigmoid(v):
    return 1.0 / (1.0 + jnp.exp2(-LOG2_E * v))


def _gelu_tanh(v):
    k1 = -2.0 * math.sqrt(2.0 / math.pi) * LOG2_E
    t = v * (k1 + (k1 * 0.044715) * (v * v))
    return v / (1.0 + jnp.exp2(t))


def _const_spec(shape):
    nd = len(shape)
    return pl.BlockSpec(shape, lambda *_: (0,) * nd)


def _ffn_kernel(x_ref, gin_ref, gout_ref, wg_ref, wu_ref, wd_ref, o_ref, acc_ref):
    x = x_ref[...]
    xn = _rms(x, gin_ref[...]).astype(BF16)
    d_ff = wg_ref.shape[1]
    for c in range(d_ff // FFN_CHUNK):
        sl = slice(c * FFN_CHUNK, (c + 1) * FFN_CHUNK)
        g = _dot(xn, wg_ref[:, sl])
        u = _dot(xn, wu_ref[:, sl])
        mid = (g * jax.nn.sigmoid(g) * u).astype(BF16)
        d = _dot(mid, wd_ref[sl, :])
        if c == 0:
            acc_ref[...] = d
        else:
            acc_ref[...] += d
    o_ref[...] = x + 0.5 * _rms(acc_ref[...], gout_ref[...])


def _ffn_block(h, g_in, g_out, w_gate, w_up, w_down, layer, which):
    n, d = h.shape
    d_ff = w_gate.shape[-1]
    assert n % FFN_ROWS == 0 and d_ff % FFN_CHUNK == 0
    row_spec = pl.BlockSpec((FFN_ROWS, d), lambda i: (i, 0))

    def weight_spec(rows, cols):
        return pl.BlockSpec((None, None, rows, cols), lambda i: (layer, which, 0, 0),
                            pipeline_mode=pl.Buffered(1))

    return pl.pallas_call(
        _ffn_kernel,
        name="ffn",
        grid=(n // FFN_ROWS,),
        in_specs=[
            row_spec,
            _const_spec((1, d)),
            _const_spec((1, d)),
            weight_spec(d, d_ff),
            weight_spec(d, d_ff),
            weight_spec(d_ff, d),
        ],
        out_specs=row_spec,
        out_shape=jax.ShapeDtypeStruct((n, d), F32),
        scratch_shapes=[pltpu.VMEM((FFN_ROWS, d), F32)],
        compiler_params=pltpu.CompilerParams(
            dimension_semantics=("arbitrary",),
            vmem_limit_bytes=V7X_VMEM_LIMIT_BYTES),
    )(h, g_in.reshape(1, d), g_out.reshape(1, d), w_gate, w_up, w_down)


def _scan_step(a, b, shift, idx):
    keep = idx >= shift
    a_prev = pltpu.roll(a, shift, axis=0)
    b_prev = pltpu.roll(b, shift, axis=0)
    b = jnp.where(keep, a * b_prev, 0.0) + b
    a = jnp.where(keep, a * a_prev, a)
    return a, b


def _lru_kernel(x_ref, gin_ref, gout_ref, win_ref, bin_ref, cw_ref, cb_ref, wai_ref,
                bai_ref, lam_ref, wout_ref, bout_ref, o_ref,
                xcp_ref, hs_ref, tail_ref, hcar_ref, a_ref, b_ref):
    strips, rows, lanes = a_ref.shape
    nseg = V7X_SUBLANES
    seg = rows // nseg
    pitch = xcp_ref.shape[1] // nseg
    width = strips * lanes
    blk = width // LRU_HEADS
    taps = CONV_WIDTH - 1

    @pl.when(pl.program_id(1) == 0)
    def _():
        tail_ref[...] = jnp.zeros_like(tail_ref)
        hcar_ref[...] = jnp.zeros_like(hcar_ref)

    x = x_ref[...]
    xn = _rms(x, gin_ref[...]).astype(BF16)
    proj = _dot(xn, win_ref[...]) + bin_ref[...]
    y = _gelu_tanh(proj[:, :width])

    for c in range(strips):
        for s in range(nseg):
            xcp_ref[c, s * pitch:s * pitch + seg, :] = (
                proj[s * seg:(s + 1) * seg, width + c * lanes:width + (c + 1) * lanes])

    sub = lax.broadcasted_iota(jnp.int32, (nseg, lanes), 0)
    xb_strips = []
    for c in range(strips):
        ls = slice(c * lanes, (c + 1) * lanes)
        regs = [xcp_ref[c, pl.ds(j, nseg, stride=pitch), :] for j in range(seg)]
        head = []
        for k in range(taps):
            last = regs[seg - taps + k]
            prev_tile = tail_ref[c, k * nseg:(k + 1) * nseg, :]
            head.append(pltpu.roll(jnp.where(sub == nseg - 1, prev_tile, last), 1, axis=0))
            tail_ref[c, k * nseg:(k + 1) * nseg, :] = last
        ext = head + regs
        w = [cw_ref[k:k + 1, ls] for k in range(CONV_WIDTH)]
        cb = cb_ref[:, ls]
        out = []
        for j in range(seg):
            acc = cb + w[taps] * ext[j + taps]
            for k in range(taps):
                acc = acc + w[k] * ext[j + k]
            out.append(acc)
        xb_strips.append(jnp.concatenate(out, axis=0))
    xb = jnp.concatenate(xb_strips, axis=1)

    lam = lam_ref[...]
    log_sig_lam = jnp.minimum(lam, 0.0) - jnp.log(1.0 + jnp.exp(-jnp.abs(lam)))
    log2_a_scale = (LRU_C * LOG2_E) * log_sig_lam
    for h in range(LRU_HEADS):
        cs = slice(h * blk, (h + 1) * blk)
        xbh = xb[:, cs]
        gates = _sigmoid(_dot(xbh.astype(BF16), wai_ref[h]) + bai_ref[h])
        a = jnp.exp2(gates[:, :blk] * log2_a_scale[:, cs])
        z = 1.0 - a * a
        mult = jnp.where(z > 0.0, z * lax.rsqrt(z), 0.0)
        b = mult * (gates[:, blk:] * xbh)
        for t in range(blk // lanes):
            cc = h * (blk // lanes) + t
            a_ref[cc] = a[:, t * lanes:(t + 1) * lanes]
            b_ref[cc] = b[:, t * lanes:(t + 1) * lanes]

    for c in range(strips):
        ls = slice(c * lanes, (c + 1) * lanes)
        hz = []
        az = []
        for j in range(seg):
            a_j = a_ref[c, j * nseg:(j + 1) * nseg, :]
            b_j = b_ref[c, j * nseg:(j + 1) * nseg, :]
            hz.append(b_j if j == 0 else a_j * hz[-1] + b_j)
            az.append(a_j if j == 0 else a_j * az[-1])
        a_tot, h_tot = az[-1], hz[-1]
        shift = 1
        while shift < nseg:
            a_tot, h_tot = _scan_step(a_tot, h_tot, shift, sub)
            shift *= 2
        h_in = hcar_ref[:, ls]
        seg_out = a_tot * h_in + h_tot
        hcar_ref[:, ls] = seg_out[nseg - 1:nseg, :]
        seg_in = jnp.where(sub >= 1, pltpu.roll(seg_out, 1, axis=0), h_in)
        for j in range(seg):
            hs_ref[c, pl.ds(j, nseg, stride=pitch), :] = hz[j] + az[j] * seg_in

    hseq = jnp.concatenate(
        [jnp.concatenate([hs_ref[c, s * pitch:s * pitch + seg, :] for s in range(nseg)],
                         axis=0) for c in range(strips)], axis=1)
    m = _dot((hseq * y).astype(BF16), wout_ref[...]) + bout_ref[...]
    o_ref[...] = x + _rms(m, gout_ref[...])


def _lru_block(h, batch, g_in, g_out, w_in, b_in, conv_w, conv_b, w_a, b_a, w_i, b_i,
               lam, w_out, b_out):
    n, d = h.shape
    width = w_out.shape[0]
    blk = width // LRU_HEADS
    tiles = n // batch // MIX_ROWS
    assert n == batch * tiles * MIX_ROWS
    strips = width // V7X_LANES
    seg_tiles = MIX_ROWS // V7X_SUBLANES // V7X_SUBLANES
    padded_rows = V7X_SUBLANES * V7X_SUBLANES * (seg_tiles + 1 - seg_tiles % 2)
    w_ai = jnp.concatenate([w_a, w_i], axis=-1).astype(BF16)
    b_ai = jnp.concatenate([b_a.reshape(LRU_HEADS, 1, blk),
                            b_i.reshape(LRU_HEADS, 1, blk)], axis=-1)
    row_spec = pl.BlockSpec((MIX_ROWS, d), lambda b, s: (b * tiles + s, 0))
    return pl.pallas_call(
        _lru_kernel,
        name="rglru",
        grid=(batch, tiles),
        in_specs=[
            row_spec,
            _const_spec((1, d)),
            _const_spec((1, d)),
            _const_spec((d, 2 * width)),
            _const_spec((1, 2 * width)),
            _const_spec((CONV_WIDTH, width)),
            _const_spec((1, width)),
            _const_spec((LRU_HEADS, blk, 2 * blk)),
            _const_spec((LRU_HEADS, 1, 2 * blk)),
            _const_spec((1, width)),
            _const_spec((width, d)),
            _const_spec((1, d)),
        ],
        out_specs=row_spec,
        out_shape=jax.ShapeDtypeStruct((n, d), F32),
        scratch_shapes=[
            pltpu.VMEM((strips, padded_rows, V7X_LANES), F32),
            pltpu.VMEM((strips, padded_rows, V7X_LANES), F32),
            pltpu.VMEM((strips, (CONV_WIDTH - 1) * V7X_SUBLANES, V7X_LANES), F32),
            pltpu.VMEM((1, width), F32),
            pltpu.VMEM((strips, MIX_ROWS, V7X_LANES), F32),
            pltpu.VMEM((strips, MIX_ROWS, V7X_LANES), F32),
        ],
        compiler_params=pltpu.CompilerParams(
            dimension_semantics=("arbitrary", "arbitrary"),
            vmem_limit_bytes=V7X_VMEM_LIMIT_BYTES),
    )(h, g_in.reshape(1, d), g_out.reshape(1, d), w_in.astype(BF16),
      b_in.reshape(1, -1), conv_w, conv_b.reshape(1, -1), w_ai, b_ai,
      lam.reshape(1, -1), w_out.astype(BF16), b_out.reshape(1, d))


def _attn_kernel(sink_ref, x_ref, gin_ref, gout_ref, wqkv_ref, bqkv_ref, wo_ref, bo_ref,
                 o_ref, kprev_ref, vprev_ref, att_ref):
    qd, rows = att_ref.shape
    blk = ATTN_BLOCK
    pair = 2 * HEAD_DIM
    pairs_per_kv = Q_PER_KV // 2
    first_tile = pl.program_id(1) == 0

    @pl.when(first_tile)
    def _():
        kprev_ref[...] = jnp.zeros_like(kprev_ref)
        vprev_ref[...] = jnp.zeros_like(vprev_ref)

    x = x_ref[...]
    xn = _rms(x, gin_ref[...]).astype(BF16)
    qkv = _dot(xn, wqkv_ref[...]) + bqkv_ref[...]
    q = qkv[:, :qd].astype(BF16)
    k_new = qkv[:, qd:qd + pair] * (1.0 / math.sqrt(HEAD_DIM))
    v_new = qkv[:, qd + pair:qd + 2 * pair]
    k_all = jnp.concatenate([kprev_ref[...], k_new], axis=0)
    v_all = jnp.concatenate([vprev_ref[...], v_new], axis=0)
    kprev_ref[...] = k_new[rows - blk:, :]
    vprev_ref[...] = v_new[rows - blk:, :]

    def placed(t, axis):
        low = lax.broadcasted_iota(jnp.int32, t.shape, axis) < HEAD_DIM
        swapped = pltpu.roll(t, HEAD_DIM, axis=axis)
        lo = [jnp.where(low, t, 0.0), jnp.where(low, swapped, 0.0)]
        hi = [jnp.where(low, 0.0, swapped), jnp.where(low, 0.0, t)]
        return ([a.astype(BF16) for a in lo], [a.astype(BF16) for a in hi])

    k_lo, k_hi = placed(k_all, 1)
    vt_lo, vt_hi = placed(v_all.T, 0)

    cols = pairs_per_kv * blk
    kj = lax.broadcasted_iota(jnp.int32, (blk, cols), 0)
    qi = jnp.bitwise_and(lax.broadcasted_iota(jnp.int32, (blk, cols), 1), blk - 1)
    from_prev = kj > qi
    no_prev = jnp.logical_and(from_prev, first_tile)
    pair_of_col = lax.broadcasted_iota(jnp.int32, (1, cols), 1) // blk

    def sink_row(kh, parity):
        row = jnp.full((1, cols), sink_ref[kh * Q_PER_KV + parity], F32)
        for p in range(1, pairs_per_kv):
            row = jnp.where(pair_of_col == p, sink_ref[kh * Q_PER_KV + 2 * p + parity], row)
        return row

    sinks = [[sink_row(kh, parity) for parity in range(2)] for kh in range(N_KV_HEADS)]

    contract_last = (((1,), (1,)), ((), ()))
    k_placed = {(kh, 0): k_lo[kh] for kh in range(N_KV_HEADS)}
    k_placed.update({(kh, 1): k_hi[kh] for kh in range(N_KV_HEADS)})
    n_blocks = rows // blk
    for j0 in range(0, n_blocks, ATTN_BLOCKS_PER_GROUP):
        chains = [(j, kh, parity) for j in range(j0, j0 + ATTN_BLOCKS_PER_GROUP)
                  for kh in range(N_KV_HEADS) for parity in range(2)]
        qs = {(j, kh): jnp.concatenate(
            [q[j * blk:(j + 1) * blk,
               (kh * pairs_per_kv + p) * pair:(kh * pairs_per_kv + p + 1) * pair]
             for p in range(pairs_per_kv)], axis=0)
              for j in range(j0, j0 + ATTN_BLOCKS_PER_GROUP)
              for kh in range(N_KV_HEADS)}
        s = [lax.dot_general(k_placed[kh, parity][j * blk:(j + 2) * blk, :], qs[j, kh],
                             contract_last, preferred_element_type=F32)
             for j, kh, parity in chains]
        comb = [jnp.where(from_prev, t[:blk, :], t[blk:, :]) for t in s]
        comb = [jnp.where(no_prev, NEG_INF, t) if j == 0 else t
                for t, (j, _, _) in zip(comb, chains)]
        m = [jnp.maximum(jnp.max(t, axis=0, keepdims=True), sinks[kh][parity])
             for t, (_, kh, parity) in zip(comb, chains)]
        e = [jnp.exp(t - mm) for t, mm in zip(comb, m)]
        den = [jnp.sum(t, axis=0, keepdims=True) + jnp.exp(sinks[kh][parity] - mm)
               for t, mm, (_, kh, parity) in zip(e, m, chains)]
        pr = [t * (1.0 / dd) for t, dd in zip(e, den)]
        probs = [jnp.concatenate([jnp.where(from_prev, t, 0.0), jnp.where(from_prev, 0.0, t)],
                                 axis=0).astype(BF16) for t in pr]
        for i in range(0, len(chains), 2):
            j, kh, _ = chains[i]
            t0 = kh * pairs_per_kv
            rhs = jnp.concatenate(probs[i:i + 2], axis=0)
            lhs = jnp.concatenate([vt_lo[kh][:, j * blk:(j + 2) * blk],
                                   vt_hi[kh][:, j * blk:(j + 2) * blk]], axis=1)
            ot = _dot(lhs, rhs)
            for p in range(pairs_per_kv):
                att_ref[(t0 + p) * pair:(t0 + p + 1) * pair, j * blk:(j + 1) * blk] = (
                    ot[:, p * blk:(p + 1) * blk])

    att = att_ref[...].T.astype(BF16)
    m = _dot(att, wo_ref[...]) + bo_ref[...]
    o_ref[...] = x + _rms(m, gout_ref[...])


def _attn_block(h, batch, g_in, g_out, w_qkv, b_qkv, sinks, w_o, b_o):
    n, d = h.shape
    qd = w_o.shape[0]
    qkv_w = w_qkv.shape[1]
    tiles = n // batch // MIX_ROWS
    assert n == batch * tiles * MIX_ROWS and MIX_ROWS % ATTN_BLOCK == 0
    assert N_KV_HEADS * HEAD_DIM == V7X_LANES
    row_spec = pl.BlockSpec((MIX_ROWS, d), lambda b, s: (b * tiles + s, 0))
    return pl.pallas_call(
        _attn_kernel,
        name="swa",
        grid=(batch, tiles),
        in_specs=[
            pl.BlockSpec(memory_space=pltpu.SMEM),
            row_spec,
            _const_spec((1, d)),
            _const_spec((1, d)),
            _const_spec((d, qkv_w)),
            _const_spec((1, qkv_w)),
            _const_spec((qd, d)),
            _const_spec((1, d)),
        ],
        out_specs=row_spec,
        out_shape=jax.ShapeDtypeStruct((n, d), F32),
        scratch_shapes=[
            pltpu.VMEM((ATTN_BLOCK, V7X_LANES), F32),
            pltpu.VMEM((ATTN_BLOCK, V7X_LANES), F32),
            pltpu.VMEM((qd, MIX_ROWS), F32),
        ],
        compiler_params=pltpu.CompilerParams(
            dimension_semantics=("arbitrary", "arbitrary"),
            vmem_limit_bytes=V7X_VMEM_LIMIT_BYTES),
    )(sinks, h, g_in.reshape(1, d), g_out.reshape(1, d), w_qkv.astype(BF16),
      b_qkv.reshape(1, qkv_w), w_o.astype(BF16), b_o.reshape(1, d))


def kernel(x, norm_ffn, norm_mix, ffn_w_gate, ffn_w_up, ffn_w_down, lru_w_in, lru_b_in, lru_conv_w, lru_conv_b, lru_w_a, lru_b_a, lru_w_i, lru_b_i, lru_lambda, lru_w_out, lru_b_out, attn_w_qkv, attn_b_qkv, attn_sinks, attn_w_o, attn_b_o):
    batch, seq, d = x.shape
    depth = norm_ffn.shape[0]
    h = x.reshape(batch * seq, d)
    w_gate, w_up, w_down = (w.astype(BF16) for w in (ffn_w_gate, ffn_w_up, ffn_w_down))
    for layer in range(depth):
        h = _ffn_block(h, norm_ffn[layer, 0, 0], norm_ffn[layer, 0, 1],
                       w_gate, w_up, w_down, layer, 0)
        j = layer // 2
        if layer % 2 == 0:
            h = _lru_block(h, batch, norm_mix[layer, 0], norm_mix[layer, 1],
                           lru_w_in[j], lru_b_in[j], lru_conv_w[j], lru_conv_b[j],
                           lru_w_a[j], lru_b_a[j], lru_w_i[j], lru_b_i[j],
                           lru_lambda[j], lru_w_out[j], lru_b_out[j])
        else:
            h = _attn_block(h, batch, norm_mix[layer, 0], norm_mix[layer, 1],
                            attn_w_qkv[j], attn_b_qkv[j], attn_sinks[j],
                            attn_w_o[j], attn_b_o[j])
        h = _ffn_block(h, norm_ffn[layer, 1, 0], norm_ffn[layer, 1, 1],
                       w_gate, w_up, w_down, layer, 1)
    return h.reshape(batch, seq, d)
```

```python
import functools
import math

import jax
import jax.numpy as jnp
from jax import lax
from jax.experimental import pallas as pl
from jax.experimental.pallas import tpu as pltpu

F32 = jnp.float32
BF16 = jnp.bfloat16

RMS_EPS = 1e-6
LOG2_E = math.log2(math.e)
NEG_INF = -1e30
LRU_C = 8.0
LRU_HEADS = 4
CONV_WIDTH = 4
HEAD_DIM = 64
N_KV_HEADS = 2
Q_PER_KV = 8
ATTN_BLOCK = 128

V7X_SUBLANES = 8
V7X_LANES = 128
V7X_MXU_DIM = 256
V7X_VMEM_LIMIT_BYTES = 56 * 1024 * 1024

FFN_ROWS = 512
FFN_CHUNK = V7X_MXU_DIM
FFN_STAGE_ROWS = 128
MIX_ROWS = 512
ATTN_BLOCKS_PER_GROUP = 2


def _rms(x, g):
    ms = jnp.mean(x * x, axis=-1, keepdims=True)
    return x * lax.rsqrt(ms + RMS_EPS) * g


def _dot(a, b):
    return jnp.dot(a, b, preferred_element_type=F32)


def _sigmoid(v):
    return 1.0 / (1.0 + jnp.exp2(-LOG2_E * v))


def _gelu_tanh(v):
    k1 = -2.0 * math.sqrt(2.0 / math.pi) * LOG2_E
    t = v * (k1 + (k1 * 0.044715) * (v * v))
    return v / (1.0 + jnp.exp2(t))


def _const_spec(shape):
    nd = len(shape)
    return pl.BlockSpec(shape, lambda *_: (0,) * nd)


def _load_cast(src_ref, dst_ref, stage_ref, sem_ref):
    rows = stage_ref.shape[1]
    n_chunks = src_ref.shape[0] // rows

    def copy(k):
        return pltpu.make_async_copy(src_ref.at[pl.ds(k * rows, rows), :],
                                     stage_ref.at[k % 2], sem_ref.at[k % 2])

    copy(0).start()
    for k in range(n_chunks):
        if k + 1 < n_chunks:
            copy(k + 1).start()
        copy(k).wait()
        dst_ref[pl.ds(k * rows, rows), :] = stage_ref[k % 2].astype(BF16)


def _ffn_kernel(x_ref, gin_ref, gout_ref, wg_hbm, wu_hbm, wd_hbm, o_ref,
                acc_ref, wg_ref, wu_ref, wd_ref, stage_in_ref, stage_out_ref, sem_ref,
                *, layer, which):
    @pl.when(pl.program_id(0) == 0)
    def _():
        _load_cast(wg_hbm.at[layer, which], wg_ref, stage_in_ref, sem_ref)
        _load_cast(wu_hbm.at[layer, which], wu_ref, stage_in_ref, sem_ref)
        _load_cast(wd_hbm.at[layer, which], wd_ref, stage_out_ref, sem_ref)

    x = x_ref[...]
    xn = _rms(x, gin_ref[...]).astype(BF16)
    d_ff = wg_ref.shape[1]
    for c in range(d_ff // FFN_CHUNK):
        sl = slice(c * FFN_CHUNK, (c + 1) * FFN_CHUNK)
        g = _dot(xn, wg_ref[:, sl])
        u = _dot(xn, wu_ref[:, sl])
        mid = (g * jax.nn.sigmoid(g) * u).astype(BF16)
        d = _dot(mid, wd_ref[sl, :])
        if c == 0:
            acc_ref[...] = d
        else:
            acc_ref[...] += d
    o_ref[...] = x + 0.5 * _rms(acc_ref[...], gout_ref[...])


def _ffn_block(h, g_in, g_out, w_gate, w_up, w_down, layer, which):
    n, d = h.shape
    d_ff = w_gate.shape[-1]
    assert n % FFN_ROWS == 0 and d_ff % FFN_CHUNK == 0
    assert d % FFN_STAGE_ROWS == 0 and d_ff % FFN_STAGE_ROWS == 0
    row_spec = pl.BlockSpec((FFN_ROWS, d), lambda i: (i, 0))
    hbm_spec = pl.BlockSpec(memory_space=pl.ANY)
    return pl.pallas_call(
        functools.partial(_ffn_kernel, layer=layer, which=which),
        name="ffn",
        grid=(n // FFN_ROWS,),
        in_specs=[
            row_spec,
            _const_spec((1, d)),
            _const_spec((1, d)),
            hbm_spec,
            hbm_spec,
            hbm_spec,
        ],
        out_specs=row_spec,
        out_shape=jax.ShapeDtypeStruct((n, d), F32),
        scratch_shapes=[
            pltpu.VMEM((FFN_ROWS, d), F32),
            pltpu.VMEM((d, d_ff), BF16),
            pltpu.VMEM((d, d_ff), BF16),
            pltpu.VMEM((d_ff, d), BF16),
            pltpu.VMEM((2, FFN_STAGE_ROWS, d_ff), F32),
            pltpu.VMEM((2, FFN_STAGE_ROWS, d), F32),
            pltpu.SemaphoreType.DMA((2,)),
        ],
        compiler_params=pltpu.CompilerParams(
            dimension_semantics=("arbitrary",),
            vmem_limit_bytes=V7X_VMEM_LIMIT_BYTES),
    )(h, g_in.reshape(1, d), g_out.reshape(1, d), w_gate, w_up, w_down)


def _scan_step(a, b, shift, idx):
    keep = idx >= shift
    a_prev = pltpu.roll(a, shift, axis=0)
    b_prev = pltpu.roll(b, shift, axis=0)
    b = jnp.where(keep, a * b_prev, 0.0) + b
    a = jnp.where(keep, a * a_prev, a)
    return a, b


def _lru_kernel(x_ref, gin_ref, gout_ref, win_ref, bin_ref, cw_ref, cb_ref, wai_ref,
                bai_ref, lam_ref, wout_ref, bout_ref, o_ref,
                xcp_ref, hs_ref, tail_ref, hcar_ref, a_ref, b_ref):
    strips, rows, lanes = a_ref.shape
    nseg = V7X_SUBLANES
    seg = rows // nseg
    pitch = xcp_ref.shape[1] // nseg
    width = strips * lanes
    blk = width // LRU_HEADS
    taps = CONV_WIDTH - 1

    @pl.when(pl.program_id(1) == 0)
    def _():
        tail_ref[...] = jnp.zeros_like(tail_ref)
        hcar_ref[...] = jnp.zeros_like(hcar_ref)

    x = x_ref[...]
    xn = _rms(x, gin_ref[...]).astype(BF16)
    proj = _dot(xn, win_ref[...]) + bin_ref[...]
    y = _gelu_tanh(proj[:, :width])

    for c in range(strips):
        for s in range(nseg):
            xcp_ref[c, s * pitch:s * pitch + seg, :] = (
                proj[s * seg:(s + 1) * seg, width + c * lanes:width + (c + 1) * lanes])

    sub = lax.broadcasted_iota(jnp.int32, (nseg, lanes), 0)
    xb_strips = []
    for c in range(strips):
        ls = slice(c * lanes, (c + 1) * lanes)
        regs = [xcp_ref[c, pl.ds(j, nseg, stride=pitch), :] for j in range(seg)]
        head = []
        for k in range(taps):
            last = regs[seg - taps + k]
            prev_tile = tail_ref[c, k * nseg:(k + 1) * nseg, :]
            head.append(pltpu.roll(jnp.where(sub == nseg - 1, prev_tile, last), 1, axis=0))
            tail_ref[c, k * nseg:(k + 1) * nseg, :] = last
        ext = head + regs
        w = [cw_ref[k:k + 1, ls] for k in range(CONV_WIDTH)]
        cb = cb_ref[:, ls]
        out = []
        for j in range(seg):
            acc = cb + w[taps] * ext[j + taps]
            for k in range(taps):
                acc = acc + w[k] * ext[j + k]
            out.append(acc)
        xb_strips.append(jnp.concatenate(out, axis=0))
    xb = jnp.concatenate(xb_strips, axis=1)

    lam = lam_ref[...]
    log_sig_lam = jnp.minimum(lam, 0.0) - jnp.log(1.0 + jnp.exp(-jnp.abs(lam)))
    log2_a_scale = (LRU_C * LOG2_E) * log_sig_lam
    for h in range(LRU_HEADS):
        cs = slice(h * blk, (h + 1) * blk)
        xbh = xb[:, cs]
        gates = _sigmoid(_dot(xbh.astype(BF16), wai_ref[h]) + bai_ref[h])
        a = jnp.exp2(gates[:, :blk] * log2_a_scale[:, cs])
        z = 1.0 - a * a
        mult = jnp.where(z > 0.0, z * lax.rsqrt(z), 0.0)
        b = mult * (gates[:, blk:] * xbh)
        for t in range(blk // lanes):
            cc = h * (blk // lanes) + t
            a_ref[cc] = a[:, t * lanes:(t + 1) * lanes]
            b_ref[cc] = b[:, t * lanes:(t + 1) * lanes]

    for c in range(strips):
        ls = slice(c * lanes, (c + 1) * lanes)
        hz = []
        az = []
        for j in range(seg):
            a_j = a_ref[c, j * nseg:(j + 1) * nseg, :]
            b_j = b_ref[c, j * nseg:(j + 1) * nseg, :]
            hz.append(b_j if j == 0 else a_j * hz[-1] + b_j)
            az.append(a_j if j == 0 else a_j * az[-1])
        a_tot, h_tot = az[-1], hz[-1]
        shift = 1
        while shift < nseg:
            a_tot, h_tot = _scan_step(a_tot, h_tot, shift, sub)
            shift *= 2
        h_in = hcar_ref[:, ls]
        seg_out = a_tot * h_in + h_tot
        hcar_ref[:, ls] = seg_out[nseg - 1:nseg, :]
        seg_in = jnp.where(sub >= 1, pltpu.roll(seg_out, 1, axis=0), h_in)
        for j in range(seg):
            hs_ref[c, pl.ds(j, nseg, stride=pitch), :] = hz[j] + az[j] * seg_in

    hseq = jnp.concatenate(
        [jnp.concatenate([hs_ref[c, s * pitch:s * pitch + seg, :] for s in range(nseg)],
                         axis=0) for c in range(strips)], axis=1)
    m = _dot((hseq * y).astype(BF16), wout_ref[...]) + bout_ref[...]
    o_ref[...] = x + _rms(m, gout_ref[...])


def _lru_block(h, batch, g_in, g_out, w_in, b_in, conv_w, conv_b, w_a, b_a, w_i, b_i,
               lam, w_out, b_out):
    n, d = h.shape
    width = w_out.shape[0]
    blk = width // LRU_HEADS
    tiles = n // batch // MIX_ROWS
    assert n == batch * tiles * MIX_ROWS
    strips = width // V7X_LANES
    seg_tiles = MIX_ROWS // V7X_SUBLANES // V7X_SUBLANES
    padded_rows = V7X_SUBLANES * V7X_SUBLANES * (seg_tiles + 1 - seg_tiles % 2)
    w_ai = jnp.concatenate([w_a, w_i], axis=-1).astype(BF16)
    b_ai = jnp.concatenate([b_a.reshape(LRU_HEADS, 1, blk),
                            b_i.reshape(LRU_HEADS, 1, blk)], axis=-1)
    row_spec = pl.BlockSpec((MIX_ROWS, d), lambda b, s: (b * tiles + s, 0))
    return pl.pallas_call(
        _lru_kernel,
        name="rglru",
        grid=(batch, tiles),
        in_specs=[
            row_spec,
            _const_spec((1, d)),
            _const_spec((1, d)),
            _const_spec((d, 2 * width)),
            _const_spec((1, 2 * width)),
            _const_spec((CONV_WIDTH, width)),
            _const_spec((1, width)),
            _const_spec((LRU_HEADS, blk, 2 * blk)),
            _const_spec((LRU_HEADS, 1, 2 * blk)),
            _const_spec((1, width)),
            _const_spec((width, d)),
            _const_spec((1, d)),
        ],
        out_specs=row_spec,
        out_shape=jax.ShapeDtypeStruct((n, d), F32),
        scratch_shapes=[
            pltpu.VMEM((strips, padded_rows, V7X_LANES), F32),
            pltpu.VMEM((strips, padded_rows, V7X_LANES), F32),
            pltpu.VMEM((strips, (CONV_WIDTH - 1) * V7X_SUBLANES, V7X_LANES), F32),
            pltpu.VMEM((1, width), F32),
            pltpu.VMEM((strips, MIX_ROWS, V7X_LANES), F32),
            pltpu.VMEM((strips, MIX_ROWS, V7X_LANES), F32),
        ],
        compiler_params=pltpu.CompilerParams(
            dimension_semantics=("arbitrary", "arbitrary"),
            vmem_limit_bytes=V7X_VMEM_LIMIT_BYTES),
    )(h, g_in.reshape(1, d), g_out.reshape(1, d), w_in.astype(BF16),
      b_in.reshape(1, -1), conv_w, conv_b.reshape(1, -1), w_ai, b_ai,
      lam.reshape(1, -1), w_out.astype(BF16), b_out.reshape(1, d))


def _attn_kernel(sink_ref, x_ref, gin_ref, gout_ref, wqkv_ref, bqkv_ref, wo_ref, bo_ref,
                 o_ref, kprev_ref, vprev_ref, att_ref):
    qd, rows = att_ref.shape
    blk = ATTN_BLOCK
    pair = 2 * HEAD_DIM
    pairs_per_kv = Q_PER_KV // 2
    first_tile = pl.program_id(1) == 0

    @pl.when(first_tile)
    def _():
        kprev_ref[...] = jnp.zeros_like(kprev_ref)
        vprev_ref[...] = jnp.zeros_like(vprev_ref)

    x = x_ref[...]
    xn = _rms(x, gin_ref[...]).astype(BF16)
    kv = _dot(xn, wqkv_ref[:, qd:]) + bqkv_ref[:, qd:]
    q = (_dot(xn, wqkv_ref[:, :qd]) + bqkv_ref[:, :qd]).astype(BF16)
    k_new = kv[:, :pair] * (1.0 / math.sqrt(HEAD_DIM))
    v_new = kv[:, pair:]
    k_all = jnp.concatenate([kprev_ref[...], k_new], axis=0)
    v_all = jnp.concatenate([vprev_ref[...], v_new], axis=0)
    kprev_ref[...] = k_new[rows - blk:, :]
    vprev_ref[...] = v_new[rows - blk:, :]

    def placed(t, axis):
        low = lax.broadcasted_iota(jnp.int32, t.shape, axis) < HEAD_DIM
        swapped = pltpu.roll(t, HEAD_DIM, axis=axis)
        lo = [jnp.where(low, t, 0.0), jnp.where(low, swapped, 0.0)]
        hi = [jnp.where(low, 0.0, swapped), jnp.where(low, 0.0, t)]
        return ([a.astype(BF16) for a in lo], [a.astype(BF16) for a in hi])

    k_lo, k_hi = placed(k_all, 1)
    vt_lo, vt_hi = placed(v_all.T, 0)

    cols = pairs_per_kv * blk
    kj = lax.broadcasted_iota(jnp.int32, (blk, cols), 0)
    qi = jnp.bitwise_and(lax.broadcasted_iota(jnp.int32, (blk, cols), 1), blk - 1)
    from_prev = kj > qi
    no_prev = jnp.logical_and(from_prev, first_tile)
    pair_of_col = lax.broadcasted_iota(jnp.int32, (1, cols), 1) // blk

    def sink_row(kh, parity):
        row = jnp.full((1, cols), sink_ref[kh * Q_PER_KV + parity], F32)
        for p in range(1, pairs_per_kv):
            row = jnp.where(pair_of_col == p, sink_ref[kh * Q_PER_KV + 2 * p + parity], row)
        return row

    sinks = [[sink_row(kh, parity) for parity in range(2)] for kh in range(N_KV_HEADS)]

    contract_last = (((1,), (1,)), ((), ()))
    k_placed = {(kh, 0): k_lo[kh] for kh in range(N_KV_HEADS)}
    k_placed.update({(kh, 1): k_hi[kh] for kh in range(N_KV_HEADS)})
    n_blocks = rows // blk
    for j0 in range(0, n_blocks, ATTN_BLOCKS_PER_GROUP):
        chains = [(j, kh, parity) for j in range(j0, j0 + ATTN_BLOCKS_PER_GROUP)
                  for kh in range(N_KV_HEADS) for parity in range(2)]
        qs = {(j, kh): jnp.concatenate(
            [q[j * blk:(j + 1) * blk,
               (kh * pairs_per_kv + p) * pair:(kh * pairs_per_kv + p + 1) * pair]
             for p in range(pairs_per_kv)], axis=0)
              for j in range(j0, j0 + ATTN_BLOCKS_PER_GROUP)
              for kh in range(N_KV_HEADS)}
        s = [lax.dot_general(k_placed[kh, parity][j * blk:(j + 2) * blk, :], qs[j, kh],
                             contract_last, preferred_element_type=F32)
             for j, kh, parity in chains]
        comb = [jnp.where(from_prev, t[:blk, :], t[blk:, :]) for t in s]
        comb = [jnp.where(no_prev, NEG_INF, t) if j == 0 else t
                for t, (j, _, _) in zip(comb, chains)]
        m = [jnp.maximum(jnp.max(t, axis=0, keepdims=True), sinks[kh][parity])
             for t, (_, kh, parity) in zip(comb, chains)]
        e = [jnp.exp(t - mm) for t, mm in zip(comb, m)]
        den = [jnp.sum(t, axis=0, keepdims=True) + jnp.exp(sinks[kh][parity] - mm)
               for t, mm, (_, kh, parity) in zip(e, m, chains)]
        pr = [t * (1.0 / dd) for t, dd in zip(e, den)]
        probs = [jnp.concatenate([jnp.where(from_prev, t, 0.0), jnp.where(from_prev, 0.0, t)],
                                 axis=0).astype(BF16) for t in pr]
        for i in range(0, len(chains), 2):
            j, kh, _ = chains[i]
            t0 = kh * pairs_per_kv
            rhs = jnp.concatenate(probs[i:i + 2], axis=0)
            lhs = jnp.concatenate([vt_lo[kh][:, j * blk:(j + 2) * blk],
                                   vt_hi[kh][:, j * blk:(j + 2) * blk]], axis=1)
            ot = _dot(lhs, rhs)
            for p in range(pairs_per_kv):
                att_ref[(t0 + p) * pair:(t0 + p + 1) * pair, j * blk:(j + 1) * blk] = (
                    ot[:, p * blk:(p + 1) * blk])

        gs = slice(j0 * blk, (j0 + ATTN_BLOCKS_PER_GROUP) * blk)
        att = att_ref[:, gs].T.astype(BF16)
        m = _dot(att, wo_ref[...]) + bo_ref[...]
        o_ref[gs, :] = x_ref[gs, :] + _rms(m, gout_ref[...])


def _attn_block(h, batch, g_in, g_out, w_qkv, b_qkv, sinks, w_o, b_o):
    n, d = h.shape
    qd = w_o.shape[0]
    qkv_w = w_qkv.shape[1]
    tiles = n // batch // MIX_ROWS
    assert n == batch * tiles * MIX_ROWS and MIX_ROWS % ATTN_BLOCK == 0
    assert N_KV_HEADS * HEAD_DIM == V7X_LANES
    row_spec = pl.BlockSpec((MIX_ROWS, d), lambda b, s: (b * tiles + s, 0))
    return pl.pallas_call(
        _attn_kernel,
        name="swa",
        grid=(batch, tiles),
        in_specs=[
            pl.BlockSpec(memory_space=pltpu.SMEM),
            row_spec,
            _const_spec((1, d)),
            _const_spec((1, d)),
            _const_spec((d, qkv_w)),
            _const_spec((1, qkv_w)),
            _const_spec((qd, d)),
            _const_spec((1, d)),
        ],
        out_specs=row_spec,
        out_shape=jax.ShapeDtypeStruct((n, d), F32),
        scratch_shapes=[
            pltpu.VMEM((ATTN_BLOCK, V7X_LANES), F32),
            pltpu.VMEM((ATTN_BLOCK, V7X_LANES), F32),
            pltpu.VMEM((qd, MIX_ROWS), F32),
        ],
        compiler_params=pltpu.CompilerParams(
            dimension_semantics=("arbitrary", "arbitrary"),
            vmem_limit_bytes=V7X_VMEM_LIMIT_BYTES),
    )(sinks, h, g_in.reshape(1, d), g_out.reshape(1, d), w_qkv.astype(BF16),
      b_qkv.reshape(1, qkv_w), w_o.astype(BF16), b_o.reshape(1, d))


def kernel(x, norm_ffn, norm_mix, ffn_w_gate, ffn_w_up, ffn_w_down, lru_w_in, lru_b_in, lru_conv_w, lru_conv_b, lru_w_a, lru_b_a, lru_w_i, lru_b_i, lru_lambda, lru_w_out, lru_b_out, attn_w_qkv, attn_b_qkv, attn_sinks, attn_w_o, attn_b_o):
    batch, seq, d = x.shape
    depth = norm_ffn.shape[0]
    h = x.reshape(batch * seq, d)
    for layer in range(depth):
        h = _ffn_block(h, norm_ffn[layer, 0, 0], norm_ffn[layer, 0, 1],
                       ffn_w_gate, ffn_w_up, ffn_w_down, layer, 0)
        j = layer // 2
        if layer % 2 == 0:
            h = _lru_block(h, batch, norm_mix[layer, 0], norm_mix[layer, 1],
                           lru_w_in[j], lru_b_in[j], lru_conv_w[j], lru_conv_b[j],
                           lru_w_a[j], lru_b_a[j], lru_w_i[j], lru_b_i[j],
                           lru_lambda[j], lru_w_out[j], lru_b_out[j])
        else:
            h = _attn_block(h, batch, norm_mix[layer, 0], norm_mix[layer, 1],
                            attn_w_qkv[j], attn_b_qkv[j], attn_sinks[j],
                            attn_w_o[j], attn_b_o[j])
        h = _ffn_block(h, norm_ffn[layer, 1, 0], norm_ffn[layer, 1, 1],
                       ffn_w_gate, ffn_w_up, ffn_w_down, layer, 1)
    return h.reshape(batch, seq, d)
```

```python
import functools
import math

import jax
import jax.numpy as jnp
from jax import lax
from jax.experimental import pallas as pl
from jax.experimental.pallas import tpu as pltpu

F32 = jnp.float32
BF16 = jnp.bfloat16

RMS_EPS = 1e-6
LOG2_E = math.log2(math.e)
NEG_INF = -1e30
LRU_C = 8.0
LRU_HEADS = 4
CONV_WIDTH = 4
HEAD_DIM = 64
N_KV_HEADS = 2
Q_PER_KV = 8
ATTN_BLOCK = 128

V7X_SUBLANES = 8
V7X_LANES = 128
V7X_MXU_DIM = 256
V7X_VMEM_LIMIT_BYTES = 56 * 1024 * 1024

FFN_ROWS = 512
FFN_CHUNK = V7X_MXU_DIM
FFN_STAGE_ROWS = 128
FFN_STAGE_SLOTS = 4
MIX_ROWS = 512
ATTN_BLOCKS_PER_GROUP = 2


def _rms(x, g):
    ms = jnp.mean(x * x, axis=-1, keepdims=True)
    return x * lax.rsqrt(ms + RMS_EPS) * g


def _dot(a, b):
    return jnp.dot(a, b, preferred_element_type=F32)


def _sigmoid(v):
    return 1.0 / (1.0 + jnp.exp2(-LOG2_E * v))


def _gelu_tanh(v):
    k1 = -2.0 * math.sqrt(2.0 / math.pi) * LOG2_E
    t = v * (k1 + (k1 * 0.044715) * (v * v))
    return v / (1.0 + jnp.exp2(t))


def _const_spec(shape):
    nd = len(shape)
    return pl.BlockSpec(shape, lambda *_: (0,) * nd)


def _load_cast(jobs):
    tasks = []
    used = {}
    for src_ref, dst_ref, stage_ref, sem_ref in jobs:
        slots, rows = stage_ref.shape[:2]
        for k in range(src_ref.shape[0] // rows):
            slot = used.get(id(stage_ref), 0) % slots
            used[id(stage_ref)] = used.get(id(stage_ref), 0) + 1
            copy = functools.partial(
                pltpu.make_async_copy, src_ref.at[pl.ds(k * rows, rows), :],
                stage_ref.at[slot], sem_ref.at[slot])
            tasks.append((copy, stage_ref, slot, dst_ref, pl.ds(k * rows, rows)))

    ahead = FFN_STAGE_SLOTS - 1
    for copy, *_ in tasks[:ahead]:
        copy().start()
    for t, (copy, stage_ref, slot, dst_ref, rows) in enumerate(tasks):
        if t + ahead < len(tasks):
            tasks[t + ahead][0]().start()
        copy().wait()
        dst_ref[rows, :] = stage_ref[slot].astype(BF16)


def _ffn_kernel(x_ref, gin_ref, gout_ref, wg_hbm, wu_hbm, wd_hbm, o_ref,
                acc_ref, wg_ref, wu_ref, wd_ref, stage_in_ref, stage_out_ref, sem_in_ref,
                sem_out_ref, *, layer, which):
    @pl.when(pl.program_id(0) == 0)
    def _():
        _load_cast([(wg_hbm.at[layer, which], wg_ref, stage_in_ref, sem_in_ref),
                    (wu_hbm.at[layer, which], wu_ref, stage_in_ref, sem_in_ref),
                    (wd_hbm.at[layer, which], wd_ref, stage_out_ref, sem_out_ref)])

    x = x_ref[...]
    xn = _rms(x, gin_ref[...]).astype(BF16)
    d_ff = wg_ref.shape[1]
    for c in range(d_ff // FFN_CHUNK):
        sl = slice(c * FFN_CHUNK, (c + 1) * FFN_CHUNK)
        g = _dot(xn, wg_ref[:, sl])
        u = _dot(xn, wu_ref[:, sl])
        mid = (g * jax.nn.sigmoid(g) * u).astype(BF16)
        d = _dot(mid, wd_ref[sl, :])
        if c == 0:
            acc_ref[...] = d
        else:
            acc_ref[...] += d
    o_ref[...] = x + 0.5 * _rms(acc_ref[...], gout_ref[...])


def _ffn_block(h, g_in, g_out, w_gate, w_up, w_down, layer, which):
    n, d = h.shape
    d_ff = w_gate.shape[-1]
    assert n % FFN_ROWS == 0 and d_ff % FFN_CHUNK == 0
    assert d % FFN_STAGE_ROWS == 0 and d_ff % FFN_STAGE_ROWS == 0
    row_spec = pl.BlockSpec((FFN_ROWS, d), lambda i: (i, 0))
    hbm_spec = pl.BlockSpec(memory_space=pl.ANY)
    return pl.pallas_call(
        functools.partial(_ffn_kernel, layer=layer, which=which),
        name="ffn",
        grid=(n // FFN_ROWS,),
        in_specs=[
            row_spec,
            _const_spec((1, d)),
            _const_spec((1, d)),
            hbm_spec,
            hbm_spec,
            hbm_spec,
        ],
        out_specs=row_spec,
        out_shape=jax.ShapeDtypeStruct((n, d), F32),
        scratch_shapes=[
            pltpu.VMEM((FFN_ROWS, d), F32),
            pltpu.VMEM((d, d_ff), BF16),
            pltpu.VMEM((d, d_ff), BF16),
            pltpu.VMEM((d_ff, d), BF16),
            pltpu.VMEM((FFN_STAGE_SLOTS, FFN_STAGE_ROWS, d_ff), F32),
            pltpu.VMEM((FFN_STAGE_SLOTS, FFN_STAGE_ROWS, d), F32),
            pltpu.SemaphoreType.DMA((FFN_STAGE_SLOTS,)),
            pltpu.SemaphoreType.DMA((FFN_STAGE_SLOTS,)),
        ],
        compiler_params=pltpu.CompilerParams(
            dimension_semantics=("arbitrary",),
            vmem_limit_bytes=V7X_VMEM_LIMIT_BYTES),
    )(h, g_in.reshape(1, d), g_out.reshape(1, d), w_gate, w_up, w_down)


def _scan_step(a, b, shift, idx):
    keep = idx >= shift
    a_prev = pltpu.roll(a, shift, axis=0)
    b_prev = pltpu.roll(b, shift, axis=0)
    b = jnp.where(keep, a * b_prev, 0.0) + b
    a = jnp.where(keep, a * a_prev, a)
    return a, b


def _lru_kernel(x_ref, gin_ref, gout_ref, win_ref, bin_ref, cw_ref, cb_ref, wai_ref,
                bai_ref, lam_ref, wout_ref, bout_ref, o_ref,
                xcp_ref, hs_ref, tail_ref, hcar_ref, a_ref, b_ref):
    strips, rows, lanes = a_ref.shape
    nseg = V7X_SUBLANES
    seg = rows // nseg
    pitch = xcp_ref.shape[1] // nseg
    width = strips * lanes
    blk = width // LRU_HEADS
    taps = CONV_WIDTH - 1

    @pl.when(pl.program_id(1) == 0)
    def _():
        tail_ref[...] = jnp.zeros_like(tail_ref)
        hcar_ref[...] = jnp.zeros_like(hcar_ref)

    x = x_ref[...]
    xn = _rms(x, gin_ref[...]).astype(BF16)
    proj = _dot(xn, win_ref[...]) + bin_ref[...]
    y = _gelu_tanh(proj[:, :width])

    for c in range(strips):
        for s in range(nseg):
            xcp_ref[c, s * pitch:s * pitch + seg, :] = (
                proj[s * seg:(s + 1) * seg, width + c * lanes:width + (c + 1) * lanes])

    sub = lax.broadcasted_iota(jnp.int32, (nseg, lanes), 0)
    xb_strips = []
    for c in range(strips):
        ls = slice(c * lanes, (c + 1) * lanes)
        regs = [xcp_ref[c, pl.ds(j, nseg, stride=pitch), :] for j in range(seg)]
        head = []
        for k in range(taps):
            last = regs[seg - taps + k]
            prev_tile = tail_ref[c, k * nseg:(k + 1) * nseg, :]
            head.append(pltpu.roll(jnp.where(sub == nseg - 1, prev_tile, last), 1, axis=0))
            tail_ref[c, k * nseg:(k + 1) * nseg, :] = last
        ext = head + regs
        w = [cw_ref[k:k + 1, ls] for k in range(CONV_WIDTH)]
        cb = cb_ref[:, ls]
        out = []
        for j in range(seg):
            acc = cb + w[taps] * ext[j + taps]
            for k in range(taps):
                acc = acc + w[k] * ext[j + k]
            out.append(acc)
        xb_strips.append(jnp.concatenate(out, axis=0))
    xb = jnp.concatenate(xb_strips, axis=1)

    lam = lam_ref[...]
    log_sig_lam = jnp.minimum(lam, 0.0) - jnp.log(1.0 + jnp.exp(-jnp.abs(lam)))
    log2_a_scale = (LRU_C * LOG2_E) * log_sig_lam
    for h in range(LRU_HEADS):
        cs = slice(h * blk, (h + 1) * blk)
        xbh = xb[:, cs]
        gates = _sigmoid(_dot(xbh.astype(BF16), wai_ref[h]) + bai_ref[h])
        a = jnp.exp2(gates[:, :blk] * log2_a_scale[:, cs])
        z = 1.0 - a * a
        mult = jnp.where(z > 0.0, z * lax.rsqrt(z), 0.0)
        b = mult * (gates[:, blk:] * xbh)
        for t in range(blk // lanes):
            cc = h * (blk // lanes) + t
            a_ref[cc] = a[:, t * lanes:(t + 1) * lanes]
            b_ref[cc] = b[:, t * lanes:(t + 1) * lanes]

    for c in range(strips):
        ls = slice(c * lanes, (c + 1) * lanes)
        hz = []
        az = []
        for j in range(seg):
            a_j = a_ref[c, j * nseg:(j + 1) * nseg, :]
            b_j = b_ref[c, j * nseg:(j + 1) * nseg, :]
            hz.append(b_j if j == 0 else a_j * hz[-1] + b_j)
            az.append(a_j if j == 0 else a_j * az[-1])
        a_tot, h_tot = az[-1], hz[-1]
        shift = 1
        while shift < nseg:
            a_tot, h_tot = _scan_step(a_tot, h_tot, shift, sub)
            shift *= 2
        h_in = hcar_ref[:, ls]
        seg_out = a_tot * h_in + h_tot
        hcar_ref[:, ls] = seg_out[nseg - 1:nseg, :]
        seg_in = jnp.where(sub >= 1, pltpu.roll(seg_out, 1, axis=0), h_in)
        for j in range(seg):
            hs_ref[c, pl.ds(j, nseg, stride=pitch), :] = hz[j] + az[j] * seg_in

    hseq = jnp.concatenate(
        [jnp.concatenate([hs_ref[c, s * pitch:s * pitch + seg, :] for s in range(nseg)],
                         axis=0) for c in range(strips)], axis=1)
    m = _dot((hseq * y).astype(BF16), wout_ref[...]) + bout_ref[...]
    o_ref[...] = x + _rms(m, gout_ref[...])


def _lru_block(h, batch, g_in, g_out, w_in, b_in, conv_w, conv_b, w_a, b_a, w_i, b_i,
               lam, w_out, b_out):
    n, d = h.shape
    width = w_out.shape[0]
    blk = width // LRU_HEADS
    tiles = n // batch // MIX_ROWS
    assert n == batch * tiles * MIX_ROWS
    strips = width // V7X_LANES
    seg_tiles = MIX_ROWS // V7X_SUBLANES // V7X_SUBLANES
    padded_rows = V7X_SUBLANES * V7X_SUBLANES * (seg_tiles + 1 - seg_tiles % 2)
    w_ai = jnp.concatenate([w_a, w_i], axis=-1).astype(BF16)
    b_ai = jnp.concatenate([b_a.reshape(LRU_HEADS, 1, blk),
                            b_i.reshape(LRU_HEADS, 1, blk)], axis=-1)
    row_spec = pl.BlockSpec((MIX_ROWS, d), lambda b, s: (b * tiles + s, 0))
    return pl.pallas_call(
        _lru_kernel,
        name="rglru",
        grid=(batch, tiles),
        in_specs=[
            row_spec,
            _const_spec((1, d)),
            _const_spec((1, d)),
            _const_spec((d, 2 * width)),
            _const_spec((1, 2 * width)),
            _const_spec((CONV_WIDTH, width)),
            _const_spec((1, width)),
            _const_spec((LRU_HEADS, blk, 2 * blk)),
            _const_spec((LRU_HEADS, 1, 2 * blk)),
            _const_spec((1, width)),
            _const_spec((width, d)),
            _const_spec((1, d)),
        ],
        out_specs=row_spec,
        out_shape=jax.ShapeDtypeStruct((n, d), F32),
        scratch_shapes=[
            pltpu.VMEM((strips, padded_rows, V7X_LANES), F32),
            pltpu.VMEM((strips, padded_rows, V7X_LANES), F32),
            pltpu.VMEM((strips, (CONV_WIDTH - 1) * V7X_SUBLANES, V7X_LANES), F32),
            pltpu.VMEM((1, width), F32),
            pltpu.VMEM((strips, MIX_ROWS, V7X_LANES), F32),
            pltpu.VMEM((strips, MIX_ROWS, V7X_LANES), F32),
        ],
        compiler_params=pltpu.CompilerParams(
            dimension_semantics=("arbitrary", "arbitrary"),
            vmem_limit_bytes=V7X_VMEM_LIMIT_BYTES),
    )(h, g_in.reshape(1, d), g_out.reshape(1, d), w_in.astype(BF16),
      b_in.reshape(1, -1), conv_w, conv_b.reshape(1, -1), w_ai, b_ai,
      lam.reshape(1, -1), w_out.astype(BF16), b_out.reshape(1, d))


def _attn_kernel(sink_ref, x_ref, gin_ref, gout_ref, wqkv_ref, bqkv_ref, wo_ref, bo_ref,
                 o_ref, kprev_ref, vprev_ref, att_ref):
    qd, rows = att_ref.shape
    blk = ATTN_BLOCK
    pair = 2 * HEAD_DIM
    pairs_per_kv = Q_PER_KV // 2
    first_tile = pl.program_id(1) == 0

    @pl.when(first_tile)
    def _():
        kprev_ref[...] = jnp.zeros_like(kprev_ref)
        vprev_ref[...] = jnp.zeros_like(vprev_ref)

    x = x_ref[...]
    xn = _rms(x, gin_ref[...]).astype(BF16)
    kv = _dot(xn, wqkv_ref[:, qd:]) + bqkv_ref[:, qd:]
    q = (_dot(xn, wqkv_ref[:, :qd]) + bqkv_ref[:, :qd]).astype(BF16)
    k_new = kv[:, :pair] * (1.0 / math.sqrt(HEAD_DIM))
    v_new = kv[:, pair:]
    k_all = jnp.concatenate([kprev_ref[...], k_new], axis=0)
    v_all = jnp.concatenate([vprev_ref[...], v_new], axis=0)
    kprev_ref[...] = k_new[rows - blk:, :]
    vprev_ref[...] = v_new[rows - blk:, :]

    def placed(t, axis):
        low = lax.broadcasted_iota(jnp.int32, t.shape, axis) < HEAD_DIM
        swapped = pltpu.roll(t, HEAD_DIM, axis=axis)
        lo = [jnp.where(low, t, 0.0), jnp.where(low, swapped, 0.0)]
        hi = [jnp.where(low, 0.0, swapped), jnp.where(low, 0.0, t)]
        return ([a.astype(BF16) for a in lo], [a.astype(BF16) for a in hi])

    k_lo, k_hi = placed(k_all, 1)
    vt_lo, vt_hi = placed(v_all.T, 0)

    cols = pairs_per_kv * blk
    kj = lax.broadcasted_iota(jnp.int32, (blk, cols), 0)
    qi = jnp.bitwise_and(lax.broadcasted_iota(jnp.int32, (blk, cols), 1), blk - 1)
    from_prev = kj > qi
    no_prev = jnp.logical_and(from_prev, first_tile)
    pair_of_col = lax.broadcasted_iota(jnp.int32, (1, cols), 1) // blk

    def sink_row(kh, parity):
        row = jnp.full((1, cols), sink_ref[kh * Q_PER_KV + parity], F32)
        for p in range(1, pairs_per_kv):
            row = jnp.where(pair_of_col == p, sink_ref[kh * Q_PER_KV + 2 * p + parity], row)
        return row

    sinks = [[sink_row(kh, parity) for parity in range(2)] for kh in range(N_KV_HEADS)]

    contract_last = (((1,), (1,)), ((), ()))
    k_placed = {(kh, 0): k_lo[kh] for kh in range(N_KV_HEADS)}
    k_placed.update({(kh, 1): k_hi[kh] for kh in range(N_KV_HEADS)})
    n_blocks = rows // blk
    for j0 in range(0, n_blocks, ATTN_BLOCKS_PER_GROUP):
        chains = [(j, kh, parity) for j in range(j0, j0 + ATTN_BLOCKS_PER_GROUP)
                  for kh in range(N_KV_HEADS) for parity in range(2)]
        qs = {(j, kh): jnp.concatenate(
            [q[j * blk:(j + 1) * blk,
               (kh * pairs_per_kv + p) * pair:(kh * pairs_per_kv + p + 1) * pair]
             for p in range(pairs_per_kv)], axis=0)
              for j in range(j0, j0 + ATTN_BLOCKS_PER_GROUP)
              for kh in range(N_KV_HEADS)}
        s = [lax.dot_general(k_placed[kh, parity][j * blk:(j + 2) * blk, :], qs[j, kh],
                             contract_last, preferred_element_type=F32)
             for j, kh, parity in chains]
        comb = [jnp.where(from_prev, t[:blk, :], t[blk:, :]) for t in s]
        comb = [jnp.where(no_prev, NEG_INF, t) if j == 0 else t
                for t, (j, _, _) in zip(comb, chains)]
        m = [jnp.maximum(jnp.max(t, axis=0, keepdims=True), sinks[kh][parity])
             for t, (_, kh, parity) in zip(comb, chains)]
        e = [jnp.exp(t - mm) for t, mm in zip(comb, m)]
        den = [jnp.sum(t, axis=0, keepdims=True) + jnp.exp(sinks[kh][parity] - mm)
               for t, mm, (_, kh, parity) in zip(e, m, chains)]
        pr = [t * (1.0 / dd) for t, dd in zip(e, den)]
        probs = [jnp.concatenate([jnp.where(from_prev, t, 0.0), jnp.where(from_prev, 0.0, t)],
                                 axis=0).astype(BF16) for t in pr]
        for i in range(0, len(chains), 2):
            j, kh, _ = chains[i]
            t0 = kh * pairs_per_kv
            rhs = jnp.concatenate(probs[i:i + 2], axis=0)
            lhs = jnp.concatenate([vt_lo[kh][:, j * blk:(j + 2) * blk],
                                   vt_hi[kh][:, j * blk:(j + 2) * blk]], axis=1)
            ot = _dot(lhs, rhs)
            for p in range(pairs_per_kv):
                att_ref[(t0 + p) * pair:(t0 + p + 1) * pair, j * blk:(j + 1) * blk] = (
                    ot[:, p * blk:(p + 1) * blk])

        gs = slice(j0 * blk, (j0 + ATTN_BLOCKS_PER_GROUP) * blk)
        att = att_ref[:, gs].T.astype(BF16)
        m = _dot(att, wo_ref[...]) + bo_ref[...]
        o_ref[gs, :] = x_ref[gs, :] + _rms(m, gout_ref[...])


def _attn_block(h, batch, g_in, g_out, w_qkv, b_qkv, sinks, w_o, b_o):
    n, d = h.shape
    qd = w_o.shape[0]
    qkv_w = w_qkv.shape[1]
    tiles = n // batch // MIX_ROWS
    assert n == batch * tiles * MIX_ROWS and MIX_ROWS % ATTN_BLOCK == 0
    assert N_KV_HEADS * HEAD_DIM == V7X_LANES
    row_spec = pl.BlockSpec((MIX_ROWS, d), lambda b, s: (b * tiles + s, 0))
    return pl.pallas_call(
        _attn_kernel,
        name="swa",
        grid=(batch, tiles),
        in_specs=[
            pl.BlockSpec(memory_space=pltpu.SMEM),
            row_spec,
            _const_spec((1, d)),
            _const_spec((1, d)),
            _const_spec((d, qkv_w)),
            _const_spec((1, qkv_w)),
            _const_spec((qd, d)),
            _const_spec((1, d)),
        ],
        out_specs=row_spec,
        out_shape=jax.ShapeDtypeStruct((n, d), F32),
        scratch_shapes=[
            pltpu.VMEM((ATTN_BLOCK, V7X_LANES), F32),
            pltpu.VMEM((ATTN_BLOCK, V7X_LANES), F32),
            pltpu.VMEM((qd, MIX_ROWS), F32),
        ],
        compiler_params=pltpu.CompilerParams(
            dimension_semantics=("arbitrary", "arbitrary"),
            vmem_limit_bytes=V7X_VMEM_LIMIT_BYTES),
    )(sinks, h, g_in.reshape(1, d), g_out.reshape(1, d), w_qkv.astype(BF16),
      b_qkv.reshape(1, qkv_w), w_o.astype(BF16), b_o.reshape(1, d))


def kernel(x, norm_ffn, norm_mix, ffn_w_gate, ffn_w_up, ffn_w_down, lru_w_in, lru_b_in, lru_conv_w, lru_conv_b, lru_w_a, lru_b_a, lru_w_i, lru_b_i, lru_lambda, lru_w_out, lru_b_out, attn_w_qkv, attn_b_qkv, attn_sinks, attn_w_o, attn_b_o):
    batch, seq, d = x.shape
    depth = norm_ffn.shape[0]
    h = x.reshape(batch * seq, d)
    for layer in range(depth):
        h = _ffn_block(h, norm_ffn[layer, 0, 0], norm_ffn[layer, 0, 1],
                       ffn_w_gate, ffn_w_up, ffn_w_down, layer, 0)
        j = layer // 2
        if layer % 2 == 0:
            h = _lru_block(h, batch, norm_mix[layer, 0], norm_mix[layer, 1],
                           lru_w_in[j], lru_b_in[j], lru_conv_w[j], lru_conv_b[j],
                           lru_w_a[j], lru_b_a[j], lru_w_i[j], lru_b_i[j],
                           lru_lambda[j], lru_w_out[j], lru_b_out[j])
        else:
            h = _attn_block(h, batch, norm_mix[layer, 0], norm_mix[layer, 1],
                            attn_w_qkv[j], attn_b_qkv[j], attn_sinks[j],
                            attn_w_o[j], attn_b_o[j])
        h = _ffn_block(h, norm_ffn[layer, 1, 0], norm_ffn[layer, 1, 1],
                       ffn_w_gate, ffn_w_up, ffn_w_down, layer, 1)
    return h.reshape(batch, seq, d)
```

```python
import functools
import math

import jax
import jax.numpy as jnp
from jax import lax
from jax.experimental import pallas as pl
from jax.experimental.pallas import tpu as pltpu

F32 = jnp.float32
BF16 = jnp.bfloat16

RMS_EPS = 1e-6
LOG2_E = math.log2(math.e)
NEG_INF = -1e30
LRU_C = 8.0
LRU_HEADS = 4
CONV_WIDTH = 4
HEAD_DIM = 64
N_KV_HEADS = 2
Q_PER_KV = 8
ATTN_BLOCK = 128

V7X_SUBLANES = 8
V7X_LANES = 128
V7X_MXU_DIM = 256
V7X_VMEM_LIMIT_BYTES = 56 * 1024 * 1024

FFN_ROWS = 1024
FFN_CHUNK = V7X_MXU_DIM
FFN_STAGE_ROWS = 128
FFN_STAGE_SLOTS = 4
MIX_ROWS = 512
ATTN_BLOCKS_PER_GROUP = 2


def _rms(x, g):
    ms = jnp.mean(x * x, axis=-1, keepdims=True)
    return x * lax.rsqrt(ms + RMS_EPS) * g


def _dot(a, b):
    return jnp.dot(a, b, preferred_element_type=F32)


def _sigmoid(v):
    return 1.0 / (1.0 + jnp.exp2(-LOG2_E * v))


def _gelu_tanh(v):
    k1 = -2.0 * math.sqrt(2.0 / math.pi) * LOG2_E
    t = v * (k1 + (k1 * 0.044715) * (v * v))
    return v / (1.0 + jnp.exp2(t))


def _const_spec(shape):
    nd = len(shape)
    return pl.BlockSpec(shape, lambda *_: (0,) * nd)


def _load_cast(jobs):
    tasks = []
    used = {}
    for src_ref, dst_ref, stage_ref, sem_ref in jobs:
        slots, rows = stage_ref.shape[:2]
        for k in range(src_ref.shape[0] // rows):
            slot = used.get(id(stage_ref), 0) % slots
            used[id(stage_ref)] = used.get(id(stage_ref), 0) + 1
            copy = functools.partial(
                pltpu.make_async_copy, src_ref.at[pl.ds(k * rows, rows), :],
                stage_ref.at[slot], sem_ref.at[slot])
            tasks.append((copy, stage_ref, slot, dst_ref, pl.ds(k * rows, rows)))

    ahead = FFN_STAGE_SLOTS - 1
    for copy, *_ in tasks[:ahead]:
        copy().start()
    for t, (copy, stage_ref, slot, dst_ref, rows) in enumerate(tasks):
        if t + ahead < len(tasks):
            tasks[t + ahead][0]().start()
        copy().wait()
        dst_ref[rows, :] = stage_ref[slot].astype(BF16)


def _ffn_kernel(x_ref, gin_ref, gout_ref, wg_hbm, wu_hbm, wd_hbm, o_ref,
                mid_ref, wg_ref, wu_ref, wd_ref, stage_in_ref, stage_out_ref, sem_in_ref,
                sem_out_ref, *, layer, which):
    @pl.when(pl.program_id(0) == 0)
    def _():
        _load_cast([(wg_hbm.at[layer, which], wg_ref, stage_in_ref, sem_in_ref),
                    (wu_hbm.at[layer, which], wu_ref, stage_in_ref, sem_in_ref),
                    (wd_hbm.at[layer, which], wd_ref, stage_out_ref, sem_out_ref)])

    x = x_ref[...]
    xn = _rms(x, gin_ref[...]).astype(BF16)
    d_ff = wg_ref.shape[1]
    for c in range(d_ff // FFN_CHUNK):
        sl = slice(c * FFN_CHUNK, (c + 1) * FFN_CHUNK)
        g = _dot(xn, wg_ref[:, sl])
        u = _dot(xn, wu_ref[:, sl])
        mid_ref[:, sl] = (g * jax.nn.sigmoid(g) * u).astype(BF16)
    f = _dot(mid_ref[...], wd_ref[...])
    o_ref[...] = x + _rms(f, 0.5 * gout_ref[...])


def _ffn_block(h, g_in, g_out, w_gate, w_up, w_down, layer, which):
    n, d = h.shape
    d_ff = w_gate.shape[-1]
    assert n % FFN_ROWS == 0 and d_ff % FFN_CHUNK == 0
    assert d % FFN_STAGE_ROWS == 0 and d_ff % FFN_STAGE_ROWS == 0
    row_spec = pl.BlockSpec((FFN_ROWS, d), lambda i: (i, 0))
    hbm_spec = pl.BlockSpec(memory_space=pl.ANY)
    return pl.pallas_call(
        functools.partial(_ffn_kernel, layer=layer, which=which),
        name="ffn",
        grid=(n // FFN_ROWS,),
        in_specs=[
            row_spec,
            _const_spec((1, d)),
            _const_spec((1, d)),
            hbm_spec,
            hbm_spec,
            hbm_spec,
        ],
        out_specs=row_spec,
        out_shape=jax.ShapeDtypeStruct((n, d), F32),
        scratch_shapes=[
            pltpu.VMEM((FFN_ROWS, d_ff), BF16),
            pltpu.VMEM((d, d_ff), BF16),
            pltpu.VMEM((d, d_ff), BF16),
            pltpu.VMEM((d_ff, d), BF16),
            pltpu.VMEM((FFN_STAGE_SLOTS, FFN_STAGE_ROWS, d_ff), F32),
            pltpu.VMEM((FFN_STAGE_SLOTS, FFN_STAGE_ROWS, d), F32),
            pltpu.SemaphoreType.DMA((FFN_STAGE_SLOTS,)),
            pltpu.SemaphoreType.DMA((FFN_STAGE_SLOTS,)),
        ],
        compiler_params=pltpu.CompilerParams(
            dimension_semantics=("arbitrary",),
            vmem_limit_bytes=V7X_VMEM_LIMIT_BYTES),
    )(h, g_in.reshape(1, d), g_out.reshape(1, d), w_gate, w_up, w_down)


def _scan_step(a, b, shift, idx):
    keep = idx >= shift
    a_prev = pltpu.roll(a, shift, axis=0)
    b_prev = pltpu.roll(b, shift, axis=0)
    b = jnp.where(keep, a * b_prev, 0.0) + b
    a = jnp.where(keep, a * a_prev, a)
    return a, b


def _lru_kernel(x_ref, gin_ref, gout_ref, win_ref, bin_ref, cw_ref, cb_ref, wai_ref,
                bai_ref, lam_ref, wout_ref, bout_ref, o_ref,
                xcp_ref, hs_ref, tail_ref, hcar_ref, a_ref, b_ref):
    strips, rows, lanes = a_ref.shape
    nseg = V7X_SUBLANES
    seg = rows // nseg
    pitch = xcp_ref.shape[1] // nseg
    width = strips * lanes
    blk = width // LRU_HEADS
    taps = CONV_WIDTH - 1

    @pl.when(pl.program_id(1) == 0)
    def _():
        tail_ref[...] = jnp.zeros_like(tail_ref)
        hcar_ref[...] = jnp.zeros_like(hcar_ref)

    x = x_ref[...]
    xn = _rms(x, gin_ref[...]).astype(BF16)
    proj = _dot(xn, win_ref[...]) + bin_ref[...]
    y = _gelu_tanh(proj[:, :width])

    for c in range(strips):
        for s in range(nseg):
            xcp_ref[c, s * pitch:s * pitch + seg, :] = (
                proj[s * seg:(s + 1) * seg, width + c * lanes:width + (c + 1) * lanes])

    sub = lax.broadcasted_iota(jnp.int32, (nseg, lanes), 0)
    xb_strips = []
    for c in range(strips):
        ls = slice(c * lanes, (c + 1) * lanes)
        regs = [xcp_ref[c, pl.ds(j, nseg, stride=pitch), :] for j in range(seg)]
        head = []
        for k in range(taps):
            last = regs[seg - taps + k]
            prev_tile = tail_ref[c, k * nseg:(k + 1) * nseg, :]
            head.append(pltpu.roll(jnp.where(sub == nseg - 1, prev_tile, last), 1, axis=0))
            tail_ref[c, k * nseg:(k + 1) * nseg, :] = last
        ext = head + regs
        w = [cw_ref[k:k + 1, ls] for k in range(CONV_WIDTH)]
        cb = cb_ref[:, ls]
        out = []
        for j in range(seg):
            acc = cb + w[taps] * ext[j + taps]
            for k in range(taps):
                acc = acc + w[k] * ext[j + k]
            out.append(acc)
        xb_strips.append(jnp.concatenate(out, axis=0))
    xb = jnp.concatenate(xb_strips, axis=1)

    lam = lam_ref[...]
    log_sig_lam = jnp.minimum(lam, 0.0) - jnp.log(1.0 + jnp.exp(-jnp.abs(lam)))
    log2_a_scale = (LRU_C * LOG2_E) * log_sig_lam
    for h in range(LRU_HEADS):
        cs = slice(h * blk, (h + 1) * blk)
        xbh = xb[:, cs]
        gates = _sigmoid(_dot(xbh.astype(BF16), wai_ref[h]) + bai_ref[h])
        a = jnp.exp2(gates[:, :blk] * log2_a_scale[:, cs])
        z = 1.0 - a * a
        mult = jnp.where(z > 0.0, z * lax.rsqrt(z), 0.0)
        b = mult * (gates[:, blk:] * xbh)
        for t in range(blk // lanes):
            cc = h * (blk // lanes) + t
            a_ref[cc] = a[:, t * lanes:(t + 1) * lanes]
            b_ref[cc] = b[:, t * lanes:(t + 1) * lanes]

    for c in range(strips):
        ls = slice(c * lanes, (c + 1) * lanes)
        hz = []
        az = []
        for j in range(seg):
            a_j = a_ref[c, j * nseg:(j + 1) * nseg, :]
            b_j = b_ref[c, j * nseg:(j + 1) * nseg, :]
            hz.append(b_j if j == 0 else a_j * hz[-1] + b_j)
            az.append(a_j if j == 0 else a_j * az[-1])
        a_tot, h_tot = az[-1], hz[-1]
        shift = 1
        while shift < nseg:
            a_tot, h_tot = _scan_step(a_tot, h_tot, shift, sub)
            shift *= 2
        h_in = hcar_ref[:, ls]
        seg_out = a_tot * h_in + h_tot
        hcar_ref[:, ls] = seg_out[nseg - 1:nseg, :]
        seg_in = jnp.where(sub >= 1, pltpu.roll(seg_out, 1, axis=0), h_in)
        for j in range(seg):
            hs_ref[c, pl.ds(j, nseg, stride=pitch), :] = hz[j] + az[j] * seg_in

    hseq = jnp.concatenate(
        [jnp.concatenate([hs_ref[c, s * pitch:s * pitch + seg, :] for s in range(nseg)],
                         axis=0) for c in range(strips)], axis=1)
    m = _dot((hseq * y).astype(BF16), wout_ref[...]) + bout_ref[...]
    o_ref[...] = x + _rms(m, gout_ref[...])


def _lru_block(h, batch, g_in, g_out, w_in, b_in, conv_w, conv_b, w_a, b_a, w_i, b_i,
               lam, w_out, b_out):
    n, d = h.shape
    width = w_out.shape[0]
    blk = width // LRU_HEADS
    tiles = n // batch // MIX_ROWS
    assert n == batch * tiles * MIX_ROWS
    strips = width // V7X_LANES
    seg_tiles = MIX_ROWS // V7X_SUBLANES // V7X_SUBLANES
    padded_rows = V7X_SUBLANES * V7X_SUBLANES * (seg_tiles + 1 - seg_tiles % 2)
    w_ai = jnp.concatenate([w_a, w_i], axis=-1).astype(BF16)
    b_ai = jnp.concatenate([b_a.reshape(LRU_HEADS, 1, blk),
                            b_i.reshape(LRU_HEADS, 1, blk)], axis=-1)
    row_spec = pl.BlockSpec((MIX_ROWS, d), lambda b, s: (b * tiles + s, 0))
    return pl.pallas_call(
        _lru_kernel,
        name="rglru",
        grid=(batch, tiles),
        in_specs=[
            row_spec,
            _const_spec((1, d)),
            _const_spec((1, d)),
            _const_spec((d, 2 * width)),
            _const_spec((1, 2 * width)),
            _const_spec((CONV_WIDTH, width)),
            _const_spec((1, width)),
            _const_spec((LRU_HEADS, blk, 2 * blk)),
            _const_spec((LRU_HEADS, 1, 2 * blk)),
            _const_spec((1, width)),
            _const_spec((width, d)),
            _const_spec((1, d)),
        ],
        out_specs=row_spec,
        out_shape=jax.ShapeDtypeStruct((n, d), F32),
        scratch_shapes=[
            pltpu.VMEM((strips, padded_rows, V7X_LANES), F32),
            pltpu.VMEM((strips, padded_rows, V7X_LANES), F32),
            pltpu.VMEM((strips, (CONV_WIDTH - 1) * V7X_SUBLANES, V7X_LANES), F32),
            pltpu.VMEM((1, width), F32),
            pltpu.VMEM((strips, MIX_ROWS, V7X_LANES), F32),
            pltpu.VMEM((strips, MIX_ROWS, V7X_LANES), F32),
        ],
        compiler_params=pltpu.CompilerParams(
            dimension_semantics=("arbitrary", "arbitrary"),
            vmem_limit_bytes=V7X_VMEM_LIMIT_BYTES),
    )(h, g_in.reshape(1, d), g_out.reshape(1, d), w_in.astype(BF16),
      b_in.reshape(1, -1), conv_w, conv_b.reshape(1, -1), w_ai, b_ai,
      lam.reshape(1, -1), w_out.astype(BF16), b_out.reshape(1, d))


def _attn_kernel(sink_ref, x_ref, gin_ref, gout_ref, wqkv_ref, bqkv_ref, wo_ref, bo_ref,
                 o_ref, kprev_ref, vprev_ref, att_ref):
    qd, rows = att_ref.shape
    blk = ATTN_BLOCK
    pair = 2 * HEAD_DIM
    pairs_per_kv = Q_PER_KV // 2
    first_tile = pl.program_id(1) == 0

    @pl.when(first_tile)
    def _():
        kprev_ref[...] = jnp.zeros_like(kprev_ref)
        vprev_ref[...] = jnp.zeros_like(vprev_ref)

    x = x_ref[...]
    xn = _rms(x, gin_ref[...]).astype(BF16)
    kv = _dot(xn, wqkv_ref[:, qd:]) + bqkv_ref[:, qd:]
    q = (_dot(xn, wqkv_ref[:, :qd]) + bqkv_ref[:, :qd]).astype(BF16)
    k_new = kv[:, :pair] * (1.0 / math.sqrt(HEAD_DIM))
    v_new = kv[:, pair:]
    k_all = jnp.concatenate([kprev_ref[...], k_new], axis=0)
    v_all = jnp.concatenate([vprev_ref[...], v_new], axis=0)
    kprev_ref[...] = k_new[rows - blk:, :]
    vprev_ref[...] = v_new[rows - blk:, :]

    def placed(t, axis):
        low = lax.broadcasted_iota(jnp.int32, t.shape, axis) < HEAD_DIM
        swapped = pltpu.roll(t, HEAD_DIM, axis=axis)
        lo = [jnp.where(low, t, 0.0), jnp.where(low, swapped, 0.0)]
        hi = [jnp.where(low, 0.0, swapped), jnp.where(low, 0.0, t)]
        return ([a.astype(BF16) for a in lo], [a.astype(BF16) for a in hi])

    k_lo, k_hi = placed(k_all, 1)
    vt_lo, vt_hi = placed(v_all.T, 0)

    cols = pairs_per_kv * blk
    kj = lax.broadcasted_iota(jnp.int32, (blk, cols), 0)
    qi = jnp.bitwise_and(lax.broadcasted_iota(jnp.int32, (blk, cols), 1), blk - 1)
    from_prev = kj > qi
    no_prev = jnp.logical_and(from_prev, first_tile)
    pair_of_col = lax.broadcasted_iota(jnp.int32, (1, cols), 1) // blk

    def sink_row(kh, parity):
        row = jnp.full((1, cols), sink_ref[kh * Q_PER_KV + parity], F32)
        for p in range(1, pairs_per_kv):
            row = jnp.where(pair_of_col == p, sink_ref[kh * Q_PER_KV + 2 * p + parity], row)
        return row

    sinks = [[sink_row(kh, parity) for parity in range(2)] for kh in range(N_KV_HEADS)]

    contract_last = (((1,), (1,)), ((), ()))
    k_placed = {(kh, 0): k_lo[kh] for kh in range(N_KV_HEADS)}
    k_placed.update({(kh, 1): k_hi[kh] for kh in range(N_KV_HEADS)})
    n_blocks = rows // blk
    for j0 in range(0, n_blocks, ATTN_BLOCKS_PER_GROUP):
        chains = [(j, kh, parity) for j in range(j0, j0 + ATTN_BLOCKS_PER_GROUP)
                  for kh in range(N_KV_HEADS) for parity in range(2)]
        qs = {(j, kh): jnp.concatenate(
            [q[j * blk:(j + 1) * blk,
               (kh * pairs_per_kv + p) * pair:(kh * pairs_per_kv + p + 1) * pair]
             for p in range(pairs_per_kv)], axis=0)
              for j in range(j0, j0 + ATTN_BLOCKS_PER_GROUP)
              for kh in range(N_KV_HEADS)}
        s = [lax.dot_general(k_placed[kh, parity][j * blk:(j + 2) * blk, :], qs[j, kh],
                             contract_last, preferred_element_type=F32)
             for j, kh, parity in chains]
        comb = [jnp.where(from_prev, t[:blk, :], t[blk:, :]) for t in s]
        comb = [jnp.where(no_prev, NEG_INF, t) if j == 0 else t
                for t, (j, _, _) in zip(comb, chains)]
        m = [jnp.maximum(jnp.max(t, axis=0, keepdims=True), sinks[kh][parity])
             for t, (_, kh, parity) in zip(comb, chains)]
        e = [jnp.exp(t - mm) for t, mm in zip(comb, m)]
        den = [jnp.sum(t, axis=0, keepdims=True) + jnp.exp(sinks[kh][parity] - mm)
               for t, mm, (_, kh, parity) in zip(e, m, chains)]
        pr = [t * (1.0 / dd) for t, dd in zip(e, den)]
        probs = [jnp.concatenate([jnp.where(from_prev, t, 0.0), jnp.where(from_prev, 0.0, t)],
                                 axis=0).astype(BF16) for t in pr]
        for i in range(0, len(chains), 2):
            j, kh, _ = chains[i]
            t0 = kh * pairs_per_kv
            rhs = jnp.concatenate(probs[i:i + 2], axis=0)
            lhs = jnp.concatenate([vt_lo[kh][:, j * blk:(j + 2) * blk],
                                   vt_hi[kh][:, j * blk:(j + 2) * blk]], axis=1)
            ot = _dot(lhs, rhs)
            for p in range(pairs_per_kv):
                att_ref[(t0 + p) * pair:(t0 + p + 1) * pair, j * blk:(j + 1) * blk] = (
                    ot[:, p * blk:(p + 1) * blk])

        gs = slice(j0 * blk, (j0 + ATTN_BLOCKS_PER_GROUP) * blk)
        att = att_ref[:, gs].T.astype(BF16)
        m = _dot(att, wo_ref[...]) + bo_ref[...]
        o_ref[gs, :] = x_ref[gs, :] + _rms(m, gout_ref[...])


def _attn_block(h, batch, g_in, g_out, w_qkv, b_qkv, sinks, w_o, b_o):
    n, d = h.shape
    qd = w_o.shape[0]
    qkv_w = w_qkv.shape[1]
    tiles = n // batch // MIX_ROWS
    assert n == batch * tiles * MIX_ROWS and MIX_ROWS % ATTN_BLOCK == 0
    assert N_KV_HEADS * HEAD_DIM == V7X_LANES
    row_spec = pl.BlockSpec((MIX_ROWS, d), lambda b, s: (b * tiles + s, 0))
    return pl.pallas_call(
        _attn_kernel,
        name="swa",
        grid=(batch, tiles),
        in_specs=[
            pl.BlockSpec(memory_space=pltpu.SMEM),
            row_spec,
            _const_spec((1, d)),
            _const_spec((1, d)),
            _const_spec((d, qkv_w)),
            _const_spec((1, qkv_w)),
            _const_spec((qd, d)),
            _const_spec((1, d)),
        ],
        out_specs=row_spec,
        out_shape=jax.ShapeDtypeStruct((n, d), F32),
        scratch_shapes=[
            pltpu.VMEM((ATTN_BLOCK, V7X_LANES), F32),
            pltpu.VMEM((ATTN_BLOCK, V7X_LANES), F32),
            pltpu.VMEM((qd, MIX_ROWS), F32),
        ],
        compiler_params=pltpu.CompilerParams(
            dimension_semantics=("arbitrary", "arbitrary"),
            vmem_limit_bytes=V7X_VMEM_LIMIT_BYTES),
    )(sinks, h, g_in.reshape(1, d), g_out.reshape(1, d), w_qkv.astype(BF16),
      b_qkv.reshape(1, qkv_w), w_o.astype(BF16), b_o.reshape(1, d))


def kernel(x, norm_ffn, norm_mix, ffn_w_gate, ffn_w_up, ffn_w_down, lru_w_in, lru_b_in, lru_conv_w, lru_conv_b, lru_w_a, lru_b_a, lru_w_i, lru_b_i, lru_lambda, lru_w_out, lru_b_out, attn_w_qkv, attn_b_qkv, attn_sinks, attn_w_o, attn_b_o):
    batch, seq, d = x.shape
    depth = norm_ffn.shape[0]
    h = x.reshape(batch * seq, d)
    for layer in range(depth):
        h = _ffn_block(h, norm_ffn[layer, 0, 0], norm_ffn[layer, 0, 1],
                       ffn_w_gate, ffn_w_up, ffn_w_down, layer, 0)
        j = layer // 2
        if layer % 2 == 0:
            h = _lru_block(h, batch, norm_mix[layer, 0], norm_mix[layer, 1],
                           lru_w_in[j], lru_b_in[j], lru_conv_w[j], lru_conv_b[j],
                           lru_w_a[j], lru_b_a[j], lru_w_i[j], lru_b_i[j],
                           lru_lambda[j], lru_w_out[j], lru_b_out[j])
        else:
            h = _attn_block(h, batch, norm_mix[layer, 0], norm_mix[layer, 1],
                            attn_w_qkv[j], attn_b_qkv[j], attn_sinks[j],
                            attn_w_o[j], attn_b_o[j])
        h = _ffn_block(h, norm_ffn[layer, 1, 0], norm_ffn[layer, 1, 1],
                       ffn_w_gate, ffn_w_up, ffn_w_down, layer, 1)
    return h.reshape(batch, seq, d)
```

```python
import functools
import math

import jax
import jax.numpy as jnp
from jax import lax
from jax.experimental import pallas as pl
from jax.experimental.pallas import tpu as pltpu

F32 = jnp.float32
BF16 = jnp.bfloat16

RMS_EPS = 1e-6
LOG2_E = math.log2(math.e)
NEG_INF = -1e30
LRU_C = 8.0
LRU_HEADS = 4
CONV_WIDTH = 4
HEAD_DIM = 64
N_KV_HEADS = 2
Q_PER_KV = 8
ATTN_BLOCK = 128

V7X_SUBLANES = 8
V7X_LANES = 128
V7X_MXU_DIM = 256
V7X_VMEM_LIMIT_BYTES = 56 * 1024 * 1024

FFN_ROWS = 1024
FFN_CHUNK = V7X_MXU_DIM
FFN_STAGE_ROWS = 128
FFN_STAGE_SLOTS = 4
MIX_ROWS = 512
LRU_ROWS = 1024
ATTN_BLOCKS_PER_GROUP = 2


def _rms(x, g):
    ms = jnp.mean(x * x, axis=-1, keepdims=True)
    return x * lax.rsqrt(ms + RMS_EPS) * g


def _dot(a, b):
    return jnp.dot(a, b, preferred_element_type=F32)


def _sigmoid(v):
    return 1.0 / (1.0 + jnp.exp2(-LOG2_E * v))


def _gelu_tanh(v):
    k1 = -2.0 * math.sqrt(2.0 / math.pi) * LOG2_E
    t = v * (k1 + (k1 * 0.044715) * (v * v))
    return v / (1.0 + jnp.exp2(t))


def _const_spec(shape):
    nd = len(shape)
    return pl.BlockSpec(shape, lambda *_: (0,) * nd)


def _load_cast(jobs):
    tasks = []
    used = {}
    for src_ref, dst_ref, stage_ref, sem_ref in jobs:
        slots, rows = stage_ref.shape[:2]
        for k in range(src_ref.shape[0] // rows):
            slot = used.get(id(stage_ref), 0) % slots
            used[id(stage_ref)] = used.get(id(stage_ref), 0) + 1
            copy = functools.partial(
                pltpu.make_async_copy, src_ref.at[pl.ds(k * rows, rows), :],
                stage_ref.at[slot], sem_ref.at[slot])
            tasks.append((copy, stage_ref, slot, dst_ref, pl.ds(k * rows, rows)))

    ahead = FFN_STAGE_SLOTS - 1
    for copy, *_ in tasks[:ahead]:
        copy().start()
    for t, (copy, stage_ref, slot, dst_ref, rows) in enumerate(tasks):
        if t + ahead < len(tasks):
            tasks[t + ahead][0]().start()
        copy().wait()
        dst_ref[rows, :] = stage_ref[slot].astype(BF16)


def _ffn_kernel(x_ref, gin_ref, gout_ref, wg_hbm, wu_hbm, wd_hbm, o_ref,
                mid_ref, wg_ref, wu_ref, wd_ref, stage_in_ref, stage_out_ref, sem_in_ref,
                sem_out_ref, *, layer, which):
    @pl.when(pl.program_id(0) == 0)
    def _():
        _load_cast([(wg_hbm.at[layer, which], wg_ref, stage_in_ref, sem_in_ref),
                    (wu_hbm.at[layer, which], wu_ref, stage_in_ref, sem_in_ref),
                    (wd_hbm.at[layer, which], wd_ref, stage_out_ref, sem_out_ref)])

    x = x_ref[...]
    xn = _rms(x, gin_ref[...]).astype(BF16)
    d_ff = wg_ref.shape[1]
    for c in range(d_ff // FFN_CHUNK):
        sl = slice(c * FFN_CHUNK, (c + 1) * FFN_CHUNK)
        g = _dot(xn, wg_ref[:, sl])
        u = _dot(xn, wu_ref[:, sl])
        mid_ref[:, sl] = (g * jax.nn.sigmoid(g) * u).astype(BF16)
    f = _dot(mid_ref[...], wd_ref[...])
    o_ref[...] = x + _rms(f, 0.5 * gout_ref[...])


def _ffn_block(h, g_in, g_out, w_gate, w_up, w_down, layer, which):
    n, d = h.shape
    d_ff = w_gate.shape[-1]
    assert n % FFN_ROWS == 0 and d_ff % FFN_CHUNK == 0
    assert d % FFN_STAGE_ROWS == 0 and d_ff % FFN_STAGE_ROWS == 0
    row_spec = pl.BlockSpec((FFN_ROWS, d), lambda i: (i, 0))
    hbm_spec = pl.BlockSpec(memory_space=pl.ANY)
    return pl.pallas_call(
        functools.partial(_ffn_kernel, layer=layer, which=which),
        name="ffn",
        grid=(n // FFN_ROWS,),
        in_specs=[
            row_spec,
            _const_spec((1, d)),
            _const_spec((1, d)),
            hbm_spec,
            hbm_spec,
            hbm_spec,
        ],
        out_specs=row_spec,
        out_shape=jax.ShapeDtypeStruct((n, d), F32),
        scratch_shapes=[
            pltpu.VMEM((FFN_ROWS, d_ff), BF16),
            pltpu.VMEM((d, d_ff), BF16),
            pltpu.VMEM((d, d_ff), BF16),
            pltpu.VMEM((d_ff, d), BF16),
            pltpu.VMEM((FFN_STAGE_SLOTS, FFN_STAGE_ROWS, d_ff), F32),
            pltpu.VMEM((FFN_STAGE_SLOTS, FFN_STAGE_ROWS, d), F32),
            pltpu.SemaphoreType.DMA((FFN_STAGE_SLOTS,)),
            pltpu.SemaphoreType.DMA((FFN_STAGE_SLOTS,)),
        ],
        compiler_params=pltpu.CompilerParams(
            dimension_semantics=("arbitrary",),
            vmem_limit_bytes=V7X_VMEM_LIMIT_BYTES),
    )(h, g_in.reshape(1, d), g_out.reshape(1, d), w_gate, w_up, w_down)


def _scan_step(a, b, shift, idx):
    keep = idx >= shift
    a_prev = pltpu.roll(a, shift, axis=0)
    b_prev = pltpu.roll(b, shift, axis=0)
    b = jnp.where(keep, a * b_prev, 0.0) + b
    a = jnp.where(keep, a * a_prev, a)
    return a, b


def _lru_kernel(x_ref, gin_ref, gout_ref, win_ref, bin_ref, cw_ref, cb_ref, wai_ref,
                bai_ref, lam_ref, wout_ref, bout_ref, o_ref,
                xcp_ref, hs_ref, tail_ref, hcar_ref, a_ref, b_ref):
    strips, rows, lanes = a_ref.shape
    nseg = V7X_SUBLANES
    seg = rows // nseg
    pitch = xcp_ref.shape[1] // nseg
    width = strips * lanes
    blk = width // LRU_HEADS
    taps = CONV_WIDTH - 1

    @pl.when(pl.program_id(1) == 0)
    def _():
        tail_ref[...] = jnp.zeros_like(tail_ref)
        hcar_ref[...] = jnp.zeros_like(hcar_ref)

    x = x_ref[...]
    xn = _rms(x, gin_ref[...]).astype(BF16)
    proj = _dot(xn, win_ref[...]) + bin_ref[...]
    y = _gelu_tanh(proj[:, :width])

    for c in range(strips):
        for s in range(nseg):
            xcp_ref[c, s * pitch:s * pitch + seg, :] = (
                proj[s * seg:(s + 1) * seg, width + c * lanes:width + (c + 1) * lanes])

    sub = lax.broadcasted_iota(jnp.int32, (nseg, lanes), 0)
    xb_strips = []
    for c in range(strips):
        ls = slice(c * lanes, (c + 1) * lanes)
        regs = [xcp_ref[c, pl.ds(j, nseg, stride=pitch), :] for j in range(seg)]
        head = []
        for k in range(taps):
            last = regs[seg - taps + k]
            prev_tile = tail_ref[c, k * nseg:(k + 1) * nseg, :]
            head.append(pltpu.roll(jnp.where(sub == nseg - 1, prev_tile, last), 1, axis=0))
            tail_ref[c, k * nseg:(k + 1) * nseg, :] = last
        ext = head + regs
        w = [cw_ref[k:k + 1, ls] for k in range(CONV_WIDTH)]
        cb = cb_ref[:, ls]
        out = []
        for j in range(seg):
            acc = cb + w[taps] * ext[j + taps]
            for k in range(taps):
                acc = acc + w[k] * ext[j + k]
            out.append(acc)
        xb_strips.append(jnp.concatenate(out, axis=0))
    xb = jnp.concatenate(xb_strips, axis=1)

    lam = lam_ref[...]
    log_sig_lam = jnp.minimum(lam, 0.0) - jnp.log(1.0 + jnp.exp(-jnp.abs(lam)))
    log2_a_scale = (LRU_C * LOG2_E) * log_sig_lam
    for h in range(LRU_HEADS):
        cs = slice(h * blk, (h + 1) * blk)
        xbh = xb[:, cs]
        gates = _sigmoid(_dot(xbh.astype(BF16), wai_ref[h]) + bai_ref[h])
        a = jnp.exp2(gates[:, :blk] * log2_a_scale[:, cs])
        z = 1.0 - a * a
        mult = jnp.where(z > 0.0, z * lax.rsqrt(z), 0.0)
        b = mult * (gates[:, blk:] * xbh)
        for t in range(blk // lanes):
            cc = h * (blk // lanes) + t
            a_ref[cc] = a[:, t * lanes:(t + 1) * lanes]
            b_ref[cc] = b[:, t * lanes:(t + 1) * lanes]

    for c in range(strips):
        ls = slice(c * lanes, (c + 1) * lanes)
        hz = []
        az = []
        for j in range(seg):
            a_j = a_ref[c, j * nseg:(j + 1) * nseg, :]
            b_j = b_ref[c, j * nseg:(j + 1) * nseg, :]
            hz.append(b_j if j == 0 else a_j * hz[-1] + b_j)
            az.append(a_j if j == 0 else a_j * az[-1])
        a_tot, h_tot = az[-1], hz[-1]
        shift = 1
        while shift < nseg:
            a_tot, h_tot = _scan_step(a_tot, h_tot, shift, sub)
            shift *= 2
        h_in = hcar_ref[:, ls]
        seg_out = a_tot * h_in + h_tot
        hcar_ref[:, ls] = seg_out[nseg - 1:nseg, :]
        seg_in = jnp.where(sub >= 1, pltpu.roll(seg_out, 1, axis=0), h_in)
        for j in range(seg):
            hs_ref[c, pl.ds(j, nseg, stride=pitch), :] = hz[j] + az[j] * seg_in

    hseq = jnp.concatenate(
        [jnp.concatenate([hs_ref[c, s * pitch:s * pitch + seg, :] for s in range(nseg)],
                         axis=0) for c in range(strips)], axis=1)
    m = _dot((hseq * y).astype(BF16), wout_ref[...]) + bout_ref[...]
    o_ref[...] = x + _rms(m, gout_ref[...])


def _lru_block(h, batch, g_in, g_out, w_in, b_in, conv_w, conv_b, w_a, b_a, w_i, b_i,
               lam, w_out, b_out):
    n, d = h.shape
    width = w_out.shape[0]
    blk = width // LRU_HEADS
    tiles = n // batch // LRU_ROWS
    assert n == batch * tiles * LRU_ROWS
    strips = width // V7X_LANES
    seg_tiles = LRU_ROWS // V7X_SUBLANES // V7X_SUBLANES
    padded_rows = V7X_SUBLANES * V7X_SUBLANES * (seg_tiles + 1 - seg_tiles % 2)
    w_ai = jnp.concatenate([w_a, w_i], axis=-1).astype(BF16)
    b_ai = jnp.concatenate([b_a.reshape(LRU_HEADS, 1, blk),
                            b_i.reshape(LRU_HEADS, 1, blk)], axis=-1)
    row_spec = pl.BlockSpec((LRU_ROWS, d), lambda b, s: (b * tiles + s, 0))
    return pl.pallas_call(
        _lru_kernel,
        name="rglru",
        grid=(batch, tiles),
        in_specs=[
            row_spec,
            _const_spec((1, d)),
            _const_spec((1, d)),
            _const_spec((d, 2 * width)),
            _const_spec((1, 2 * width)),
            _const_spec((CONV_WIDTH, width)),
            _const_spec((1, width)),
            _const_spec((LRU_HEADS, blk, 2 * blk)),
            _const_spec((LRU_HEADS, 1, 2 * blk)),
            _const_spec((1, width)),
            _const_spec((width, d)),
            _const_spec((1, d)),
        ],
        out_specs=row_spec,
        out_shape=jax.ShapeDtypeStruct((n, d), F32),
        scratch_shapes=[
            pltpu.VMEM((strips, padded_rows, V7X_LANES), F32),
            pltpu.VMEM((strips, padded_rows, V7X_LANES), F32),
            pltpu.VMEM((strips, (CONV_WIDTH - 1) * V7X_SUBLANES, V7X_LANES), F32),
            pltpu.VMEM((1, width), F32),
            pltpu.VMEM((strips, LRU_ROWS, V7X_LANES), F32),
            pltpu.VMEM((strips, LRU_ROWS, V7X_LANES), F32),
        ],
        compiler_params=pltpu.CompilerParams(
            dimension_semantics=("arbitrary", "arbitrary"),
            vmem_limit_bytes=V7X_VMEM_LIMIT_BYTES),
    )(h, g_in.reshape(1, d), g_out.reshape(1, d), w_in.astype(BF16),
      b_in.reshape(1, -1), conv_w, conv_b.reshape(1, -1), w_ai, b_ai,
      lam.reshape(1, -1), w_out.astype(BF16), b_out.reshape(1, d))


def _attn_kernel(sink_ref, x_ref, gin_ref, gout_ref, wqkv_ref, bqkv_ref, wo_ref, bo_ref,
                 o_ref, kprev_ref, vprev_ref, att_ref):
    qd, rows = att_ref.shape
    blk = ATTN_BLOCK
    pair = 2 * HEAD_DIM
    pairs_per_kv = Q_PER_KV // 2
    first_tile = pl.program_id(1) == 0

    @pl.when(first_tile)
    def _():
        kprev_ref[...] = jnp.zeros_like(kprev_ref)
        vprev_ref[...] = jnp.zeros_like(vprev_ref)

    x = x_ref[...]
    xn = _rms(x, gin_ref[...]).astype(BF16)
    kv = _dot(xn, wqkv_ref[:, qd:]) + bqkv_ref[:, qd:]
    q = (_dot(xn, wqkv_ref[:, :qd]) + bqkv_ref[:, :qd]).astype(BF16)
    k_new = kv[:, :pair] * (1.0 / math.sqrt(HEAD_DIM))
    v_new = kv[:, pair:]
    k_all = jnp.concatenate([kprev_ref[...], k_new], axis=0)
    v_all = jnp.concatenate([vprev_ref[...], v_new], axis=0)
    kprev_ref[...] = k_new[rows - blk:, :]
    vprev_ref[...] = v_new[rows - blk:, :]

    def placed(t, axis):
        low = lax.broadcasted_iota(jnp.int32, t.shape, axis) < HEAD_DIM
        swapped = pltpu.roll(t, HEAD_DIM, axis=axis)
        lo = [jnp.where(low, t, 0.0), jnp.where(low, swapped, 0.0)]
        hi = [jnp.where(low, 0.0, swapped), jnp.where(low, 0.0, t)]
        return ([a.astype(BF16) for a in lo], [a.astype(BF16) for a in hi])

    k_lo, k_hi = placed(k_all, 1)
    vt_lo, vt_hi = placed(v_all.T, 0)

    cols = pairs_per_kv * blk
    kj = lax.broadcasted_iota(jnp.int32, (blk, cols), 0)
    qi = jnp.bitwise_and(lax.broadcasted_iota(jnp.int32, (blk, cols), 1), blk - 1)
    from_prev = kj > qi
    no_prev = jnp.logical_and(from_prev, first_tile)
    pair_of_col = lax.broadcasted_iota(jnp.int32, (1, cols), 1) // blk

    def sink_row(kh, parity):
        row = jnp.full((1, cols), sink_ref[kh * Q_PER_KV + parity], F32)
        for p in range(1, pairs_per_kv):
            row = jnp.where(pair_of_col == p, sink_ref[kh * Q_PER_KV + 2 * p + parity], row)
        return row

    sinks = [[sink_row(kh, parity) for parity in range(2)] for kh in range(N_KV_HEADS)]

    contract_last = (((1,), (1,)), ((), ()))
    k_placed = {(kh, 0): k_lo[kh] for kh in range(N_KV_HEADS)}
    k_placed.update({(kh, 1): k_hi[kh] for kh in range(N_KV_HEADS)})
    n_blocks = rows // blk
    def group_chains(j0):
        return [(j, kh, parity) for j in range(j0, j0 + ATTN_BLOCKS_PER_GROUP)
                for kh in range(N_KV_HEADS) for parity in range(2)]

    def scores(j0):
        qs = {(j, kh): jnp.concatenate(
            [q[j * blk:(j + 1) * blk,
               (kh * pairs_per_kv + p) * pair:(kh * pairs_per_kv + p + 1) * pair]
             for p in range(pairs_per_kv)], axis=0)
              for j in range(j0, j0 + ATTN_BLOCKS_PER_GROUP)
              for kh in range(N_KV_HEADS)}
        return [lax.dot_general(k_placed[kh, parity][j * blk:(j + 2) * blk, :], qs[j, kh],
                                contract_last, preferred_element_type=F32)
                for j, kh, parity in group_chains(j0)]

    group_starts = list(range(0, n_blocks, ATTN_BLOCKS_PER_GROUP))
    s_ahead = scores(group_starts[0])
    for gi, j0 in enumerate(group_starts):
        chains = group_chains(j0)
        s = s_ahead
        if gi + 1 < len(group_starts):
            s_ahead = scores(group_starts[gi + 1])
        comb = [jnp.where(from_prev, t[:blk, :], t[blk:, :]) for t in s]
        comb = [jnp.where(no_prev, NEG_INF, t) if j == 0 else t
                for t, (j, _, _) in zip(comb, chains)]
        m = [jnp.maximum(jnp.max(t, axis=0, keepdims=True), sinks[kh][parity])
             for t, (_, kh, parity) in zip(comb, chains)]
        e = [jnp.exp(t - mm) for t, mm in zip(comb, m)]
        den = [jnp.sum(t, axis=0, keepdims=True) + jnp.exp(sinks[kh][parity] - mm)
               for t, mm, (_, kh, parity) in zip(e, m, chains)]
        pr = [t * (1.0 / dd) for t, dd in zip(e, den)]
        probs = [jnp.concatenate([jnp.where(from_prev, t, 0.0), jnp.where(from_prev, 0.0, t)],
                                 axis=0).astype(BF16) for t in pr]
        for i in range(0, len(chains), 2):
            j, kh, _ = chains[i]
            t0 = kh * pairs_per_kv
            rhs = jnp.concatenate(probs[i:i + 2], axis=0)
            lhs = jnp.concatenate([vt_lo[kh][:, j * blk:(j + 2) * blk],
                                   vt_hi[kh][:, j * blk:(j + 2) * blk]], axis=1)
            ot = _dot(lhs, rhs)
            for p in range(pairs_per_kv):
                att_ref[(t0 + p) * pair:(t0 + p + 1) * pair, j * blk:(j + 1) * blk] = (
                    ot[:, p * blk:(p + 1) * blk])

        gs = slice(j0 * blk, (j0 + ATTN_BLOCKS_PER_GROUP) * blk)
        att = att_ref[:, gs].T.astype(BF16)
        m = _dot(att, wo_ref[...]) + bo_ref[...]
        o_ref[gs, :] = x_ref[gs, :] + _rms(m, gout_ref[...])


def _attn_block(h, batch, g_in, g_out, w_qkv, b_qkv, sinks, w_o, b_o):
    n, d = h.shape
    qd = w_o.shape[0]
    qkv_w = w_qkv.shape[1]
    tiles = n // batch // MIX_ROWS
    assert n == batch * tiles * MIX_ROWS and MIX_ROWS % ATTN_BLOCK == 0
    assert N_KV_HEADS * HEAD_DIM == V7X_LANES
    row_spec = pl.BlockSpec((MIX_ROWS, d), lambda b, s: (b * tiles + s, 0))
    return pl.pallas_call(
        _attn_kernel,
        name="swa",
        grid=(batch, tiles),
        in_specs=[
            pl.BlockSpec(memory_space=pltpu.SMEM),
            row_spec,
            _const_spec((1, d)),
            _const_spec((1, d)),
            _const_spec((d, qkv_w)),
            _const_spec((1, qkv_w)),
            _const_spec((qd, d)),
            _const_spec((1, d)),
        ],
        out_specs=row_spec,
        out_shape=jax.ShapeDtypeStruct((n, d), F32),
        scratch_shapes=[
            pltpu.VMEM((ATTN_BLOCK, V7X_LANES), F32),
            pltpu.VMEM((ATTN_BLOCK, V7X_LANES), F32),
            pltpu.VMEM((qd, MIX_ROWS), F32),
        ],
        compiler_params=pltpu.CompilerParams(
            dimension_semantics=("arbitrary", "arbitrary"),
            vmem_limit_bytes=V7X_VMEM_LIMIT_BYTES),
    )(sinks, h, g_in.reshape(1, d), g_out.reshape(1, d), w_qkv.astype(BF16),
      b_qkv.reshape(1, qkv_w), w_o.astype(BF16), b_o.reshape(1, d))


def kernel(x, norm_ffn, norm_mix, ffn_w_gate, ffn_w_up, ffn_w_down, lru_w_in, lru_b_in, lru_conv_w, lru_conv_b, lru_w_a, lru_b_a, lru_w_i, lru_b_i, lru_lambda, lru_w_out, lru_b_out, attn_w_qkv, attn_b_qkv, attn_sinks, attn_w_o, attn_b_o):
    batch, seq, d = x.shape
    depth = norm_ffn.shape[0]
    h = x.reshape(batch * seq, d)
    for layer in range(depth):
        h = _ffn_block(h, norm_ffn[layer, 0, 0], norm_ffn[layer, 0, 1],
                       ffn_w_gate, ffn_w_up, ffn_w_down, layer, 0)
        j = layer // 2
        if layer % 2 == 0:
            h = _lru_block(h, batch, norm_mix[layer, 0], norm_mix[layer, 1],
                           lru_w_in[j], lru_b_in[j], lru_conv_w[j], lru_conv_b[j],
                           lru_w_a[j], lru_b_a[j], lru_w_i[j], lru_b_i[j],
                           lru_lambda[j], lru_w_out[j], lru_b_out[j])
        else:
            h = _attn_block(h, batch, norm_mix[layer, 0], norm_mix[layer, 1],
                            attn_w_qkv[j], attn_b_qkv[j], attn_sinks[j],
                            attn_w_o[j], attn_b_o[j])
        h = _ffn_block(h, norm_ffn[layer, 1, 0], norm_ffn[layer, 1, 1],
                       ffn_w_gate, ffn_w_up, ffn_w_down, layer, 1)
    return h.reshape(batch, seq, d)
```

```python
import functools
import math

import jax
import jax.numpy as jnp
from jax import lax
from jax.experimental import pallas as pl
from jax.experimental.pallas import tpu as pltpu

F32 = jnp.float32
BF16 = jnp.bfloat16

RMS_EPS = 1e-6
LOG2_E = math.log2(math.e)
NEG_INF = -1e30
LRU_C = 8.0
LRU_HEADS = 4
CONV_WIDTH = 4
HEAD_DIM = 64
N_KV_HEADS = 2
Q_PER_KV = 8
ATTN_BLOCK = 128

V7X_SUBLANES = 8
V7X_LANES = 128
V7X_MXU_DIM = 256
V7X_VMEM_LIMIT_BYTES = 56 * 1024 * 1024

FFN_ROWS = 1024
FFN_CHUNK = V7X_MXU_DIM
FFN_OUT_PIECES = 4
FFN_STAGE_ROWS = 128
FFN_STAGE_SLOTS = 4
MIX_ROWS = 512
LRU_ROWS = 1024
ATTN_BLOCKS_PER_GROUP = 2


def _rms(x, g):
    ms = jnp.mean(x * x, axis=-1, keepdims=True)
    return x * lax.rsqrt(ms + RMS_EPS) * g


def _dot(a, b):
    return jnp.dot(a, b, preferred_element_type=F32)


def _sigmoid(v):
    return 1.0 / (1.0 + jnp.exp2(-LOG2_E * v))


def _gelu_tanh(v):
    k1 = -2.0 * math.sqrt(2.0 / math.pi) * LOG2_E
    t = v * (k1 + (k1 * 0.044715) * (v * v))
    return v / (1.0 + jnp.exp2(t))


def _const_spec(shape):
    nd = len(shape)
    return pl.BlockSpec(shape, lambda *_: (0,) * nd)


def _load_cast(jobs):
    tasks = []
    used = {}
    for src_ref, dst_ref, stage_ref, sem_ref in jobs:
        slots, rows = stage_ref.shape[:2]
        for k in range(src_ref.shape[0] // rows):
            slot = used.get(id(stage_ref), 0) % slots
            used[id(stage_ref)] = used.get(id(stage_ref), 0) + 1
            copy = functools.partial(
                pltpu.make_async_copy, src_ref.at[pl.ds(k * rows, rows), :],
                stage_ref.at[slot], sem_ref.at[slot])
            tasks.append((copy, stage_ref, slot, dst_ref, pl.ds(k * rows, rows)))

    ahead = FFN_STAGE_SLOTS - 1
    for copy, *_ in tasks[:ahead]:
        copy().start()
    for t, (copy, stage_ref, slot, dst_ref, rows) in enumerate(tasks):
        if t + ahead < len(tasks):
            tasks[t + ahead][0]().start()
        copy().wait()
        dst_ref[rows, :] = stage_ref[slot].astype(BF16)


def _ffn_kernel(x_ref, gin_ref, gout_ref, wg_hbm, wu_hbm, wd_hbm, o_ref,
                mid_ref, wg_ref, wu_ref, wd_ref, stage_in_ref, stage_out_ref, sem_in_ref,
                sem_out_ref, *, layer, which):
    @pl.when(pl.program_id(0) == 0)
    def _():
        _load_cast([(wg_hbm.at[layer, which], wg_ref, stage_in_ref, sem_in_ref),
                    (wu_hbm.at[layer, which], wu_ref, stage_in_ref, sem_in_ref),
                    (wd_hbm.at[layer, which], wd_ref, stage_out_ref, sem_out_ref)])

    x = x_ref[...]
    xn = _rms(x, gin_ref[...]).astype(BF16)
    d_ff = wg_ref.shape[1]
    for c in range(d_ff // FFN_CHUNK):
        sl = slice(c * FFN_CHUNK, (c + 1) * FFN_CHUNK)
        g = _dot(xn, wg_ref[:, sl])
        u = _dot(xn, wu_ref[:, sl])
        mid_ref[:, sl] = (g * jax.nn.sigmoid(g) * u).astype(BF16)
    piece = x_ref.shape[0] // FFN_OUT_PIECES
    for r in range(FFN_OUT_PIECES):
        rs = slice(r * piece, (r + 1) * piece)
        f = _dot(mid_ref[rs, :], wd_ref[...])
        o_ref[rs, :] = x_ref[rs, :] + _rms(f, 0.5 * gout_ref[...])


def _ffn_block(h, g_in, g_out, w_gate, w_up, w_down, layer, which):
    n, d = h.shape
    d_ff = w_gate.shape[-1]
    assert n % FFN_ROWS == 0 and d_ff % FFN_CHUNK == 0
    assert d % FFN_STAGE_ROWS == 0 and d_ff % FFN_STAGE_ROWS == 0
    row_spec = pl.BlockSpec((FFN_ROWS, d), lambda i: (i, 0))
    hbm_spec = pl.BlockSpec(memory_space=pl.ANY)
    return pl.pallas_call(
        functools.partial(_ffn_kernel, layer=layer, which=which),
        name="ffn",
        grid=(n // FFN_ROWS,),
        in_specs=[
            row_spec,
            _const_spec((1, d)),
            _const_spec((1, d)),
            hbm_spec,
            hbm_spec,
            hbm_spec,
        ],
        out_specs=row_spec,
        out_shape=jax.ShapeDtypeStruct((n, d), F32),
        scratch_shapes=[
            pltpu.VMEM((FFN_ROWS, d_ff), BF16),
            pltpu.VMEM((d, d_ff), BF16),
            pltpu.VMEM((d, d_ff), BF16),
            pltpu.VMEM((d_ff, d), BF16),
            pltpu.VMEM((FFN_STAGE_SLOTS, FFN_STAGE_ROWS, d_ff), F32),
            pltpu.VMEM((FFN_STAGE_SLOTS, FFN_STAGE_ROWS, d), F32),
            pltpu.SemaphoreType.DMA((FFN_STAGE_SLOTS,)),
            pltpu.SemaphoreType.DMA((FFN_STAGE_SLOTS,)),
        ],
        compiler_params=pltpu.CompilerParams(
            dimension_semantics=("arbitrary",),
            vmem_limit_bytes=V7X_VMEM_LIMIT_BYTES),
    )(h, g_in.reshape(1, d), g_out.reshape(1, d), w_gate, w_up, w_down)


def _scan_step(a, b, shift, idx):
    keep = idx >= shift
    a_prev = pltpu.roll(a, shift, axis=0)
    b_prev = pltpu.roll(b, shift, axis=0)
    b = jnp.where(keep, a * b_prev, 0.0) + b
    a = jnp.where(keep, a * a_prev, a)
    return a, b


def _lru_kernel(x_ref, gin_ref, gout_ref, win_ref, bin_ref, cw_ref, cb_ref, wai_ref,
                bai_ref, lam_ref, wout_ref, bout_ref, o_ref,
                xcp_ref, hs_ref, tail_ref, hcar_ref, a_ref, b_ref):
    strips, rows, lanes = a_ref.shape
    nseg = V7X_SUBLANES
    seg = rows // nseg
    pitch = xcp_ref.shape[1] // nseg
    width = strips * lanes
    blk = width // LRU_HEADS
    taps = CONV_WIDTH - 1

    @pl.when(pl.program_id(1) == 0)
    def _():
        tail_ref[...] = jnp.zeros_like(tail_ref)
        hcar_ref[...] = jnp.zeros_like(hcar_ref)

    x = x_ref[...]
    xn = _rms(x, gin_ref[...]).astype(BF16)
    proj = _dot(xn, win_ref[...]) + bin_ref[...]
    y = _gelu_tanh(proj[:, :width])

    for c in range(strips):
        for s in range(nseg):
            xcp_ref[c, s * pitch:s * pitch + seg, :] = (
                proj[s * seg:(s + 1) * seg, width + c * lanes:width + (c + 1) * lanes])

    sub = lax.broadcasted_iota(jnp.int32, (nseg, lanes), 0)
    xb_strips = []
    for c in range(strips):
        ls = slice(c * lanes, (c + 1) * lanes)
        regs = [xcp_ref[c, pl.ds(j, nseg, stride=pitch), :] for j in range(seg)]
        head = []
        for k in range(taps):
            last = regs[seg - taps + k]
            prev_tile = tail_ref[c, k * nseg:(k + 1) * nseg, :]
            head.append(pltpu.roll(jnp.where(sub == nseg - 1, prev_tile, last), 1, axis=0))
            tail_ref[c, k * nseg:(k + 1) * nseg, :] = last
        ext = head + regs
        w = [cw_ref[k:k + 1, ls] for k in range(CONV_WIDTH)]
        cb = cb_ref[:, ls]
        out = []
        for j in range(seg):
            acc = cb + w[taps] * ext[j + taps]
            for k in range(taps):
                acc = acc + w[k] * ext[j + k]
            out.append(acc)
        xb_strips.append(jnp.concatenate(out, axis=0))
    xb = jnp.concatenate(xb_strips, axis=1)

    lam = lam_ref[...]
    log_sig_lam = jnp.minimum(lam, 0.0) - jnp.log(1.0 + jnp.exp(-jnp.abs(lam)))
    log2_a_scale = (LRU_C * LOG2_E) * log_sig_lam
    for h in range(LRU_HEADS):
        cs = slice(h * blk, (h + 1) * blk)
        xbh = xb[:, cs]
        gates = _sigmoid(_dot(xbh.astype(BF16), wai_ref[h]) + bai_ref[h])
        a = jnp.exp2(gates[:, :blk] * log2_a_scale[:, cs])
        z = 1.0 - a * a
        mult = jnp.where(z > 0.0, z * lax.rsqrt(z), 0.0)
        b = mult * (gates[:, blk:] * xbh)
        for t in range(blk // lanes):
            cc = h * (blk // lanes) + t
            a_ref[cc] = a[:, t * lanes:(t + 1) * lanes]
            b_ref[cc] = b[:, t * lanes:(t + 1) * lanes]

    for c in range(strips):
        ls = slice(c * lanes, (c + 1) * lanes)
        hz = []
        az = []
        for j in range(seg):
            a_j = a_ref[c, j * nseg:(j + 1) * nseg, :]
            b_j = b_ref[c, j * nseg:(j + 1) * nseg, :]
            hz.append(b_j if j == 0 else a_j * hz[-1] + b_j)
            az.append(a_j if j == 0 else a_j * az[-1])
        a_tot, h_tot = az[-1], hz[-1]
        shift = 1
        while shift < nseg:
            a_tot, h_tot = _scan_step(a_tot, h_tot, shift, sub)
            shift *= 2
        h_in = hcar_ref[:, ls]
        seg_out = a_tot * h_in + h_tot
        hcar_ref[:, ls] = seg_out[nseg - 1:nseg, :]
        seg_in = jnp.where(sub >= 1, pltpu.roll(seg_out, 1, axis=0), h_in)
        for j in range(seg):
            hs_ref[c, pl.ds(j, nseg, stride=pitch), :] = hz[j] + az[j] * seg_in

    hseq = jnp.concatenate(
        [jnp.concatenate([hs_ref[c, s * pitch:s * pitch + seg, :] for s in range(nseg)],
                         axis=0) for c in range(strips)], axis=1)
    m = _dot((hseq * y).astype(BF16), wout_ref[...]) + bout_ref[...]
    o_ref[...] = x + _rms(m, gout_ref[...])


def _lru_block(h, batch, g_in, g_out, w_in, b_in, conv_w, conv_b, w_a, b_a, w_i, b_i,
               lam, w_out, b_out):
    n, d = h.shape
    width = w_out.shape[0]
    blk = width // LRU_HEADS
    tiles = n // batch // LRU_ROWS
    assert n == batch * tiles * LRU_ROWS
    strips = width // V7X_LANES
    seg_tiles = LRU_ROWS // V7X_SUBLANES // V7X_SUBLANES
    padded_rows = V7X_SUBLANES * V7X_SUBLANES * (seg_tiles + 1 - seg_tiles % 2)
    w_ai = jnp.concatenate([w_a, w_i], axis=-1).astype(BF16)
    b_ai = jnp.concatenate([b_a.reshape(LRU_HEADS, 1, blk),
                            b_i.reshape(LRU_HEADS, 1, blk)], axis=-1)
    row_spec = pl.BlockSpec((LRU_ROWS, d), lambda b, s: (b * tiles + s, 0))
    return pl.pallas_call(
        _lru_kernel,
        name="rglru",
        grid=(batch, tiles),
        in_specs=[
            row_spec,
            _const_spec((1, d)),
            _const_spec((1, d)),
            _const_spec((d, 2 * width)),
            _const_spec((1, 2 * width)),
            _const_spec((CONV_WIDTH, width)),
            _const_spec((1, width)),
            _const_spec((LRU_HEADS, blk, 2 * blk)),
            _const_spec((LRU_HEADS, 1, 2 * blk)),
            _const_spec((1, width)),
            _const_spec((width, d)),
            _const_spec((1, d)),
        ],
        out_specs=row_spec,
        out_shape=jax.ShapeDtypeStruct((n, d), F32),
        scratch_shapes=[
            pltpu.VMEM((strips, padded_rows, V7X_LANES), F32),
            pltpu.VMEM((strips, padded_rows, V7X_LANES), F32),
            pltpu.VMEM((strips, (CONV_WIDTH - 1) * V7X_SUBLANES, V7X_LANES), F32),
            pltpu.VMEM((1, width), F32),
            pltpu.VMEM((strips, LRU_ROWS, V7X_LANES), F32),
            pltpu.VMEM((strips, LRU_ROWS, V7X_LANES), F32),
        ],
        compiler_params=pltpu.CompilerParams(
            dimension_semantics=("arbitrary", "arbitrary"),
            vmem_limit_bytes=V7X_VMEM_LIMIT_BYTES),
    )(h, g_in.reshape(1, d), g_out.reshape(1, d), w_in.astype(BF16),
      b_in.reshape(1, -1), conv_w, conv_b.reshape(1, -1), w_ai, b_ai,
      lam.reshape(1, -1), w_out.astype(BF16), b_out.reshape(1, d))


def _attn_kernel(sink_ref, x_ref, gin_ref, gout_ref, wqkv_ref, bqkv_ref, wo_ref, bo_ref,
                 o_ref, kprev_ref, vprev_ref, att_ref):
    qd, rows = att_ref.shape
    blk = ATTN_BLOCK
    pair = 2 * HEAD_DIM
    pairs_per_kv = Q_PER_KV // 2
    first_tile = pl.program_id(1) == 0

    @pl.when(first_tile)
    def _():
        kprev_ref[...] = jnp.zeros_like(kprev_ref)
        vprev_ref[...] = jnp.zeros_like(vprev_ref)

    x = x_ref[...]
    xn = _rms(x, gin_ref[...]).astype(BF16)
    kv = _dot(xn, wqkv_ref[:, qd:]) + bqkv_ref[:, qd:]
    q = (_dot(xn, wqkv_ref[:, :qd]) + bqkv_ref[:, :qd]).astype(BF16)
    k_new = kv[:, :pair] * (1.0 / math.sqrt(HEAD_DIM))
    v_new = kv[:, pair:]
    k_all = jnp.concatenate([kprev_ref[...], k_new], axis=0)
    v_all = jnp.concatenate([vprev_ref[...], v_new], axis=0)
    kprev_ref[...] = k_new[rows - blk:, :]
    vprev_ref[...] = v_new[rows - blk:, :]

    def placed(t, axis):
        low = lax.broadcasted_iota(jnp.int32, t.shape, axis) < HEAD_DIM
        swapped = pltpu.roll(t, HEAD_DIM, axis=axis)
        lo = [jnp.where(low, t, 0.0), jnp.where(low, swapped, 0.0)]
        hi = [jnp.where(low, 0.0, swapped), jnp.where(low, 0.0, t)]
        return ([a.astype(BF16) for a in lo], [a.astype(BF16) for a in hi])

    k_lo, k_hi = placed(k_all, 1)
    vt_lo, vt_hi = placed(v_all.T, 0)

    cols = pairs_per_kv * blk
    kj = lax.broadcasted_iota(jnp.int32, (blk, cols), 0)
    qi = jnp.bitwise_and(lax.broadcasted_iota(jnp.int32, (blk, cols), 1), blk - 1)
    from_prev = kj > qi
    no_prev = jnp.logical_and(from_prev, first_tile)
    pair_of_col = lax.broadcasted_iota(jnp.int32, (1, cols), 1) // blk

    def sink_row(kh, parity):
        row = jnp.full((1, cols), sink_ref[kh * Q_PER_KV + parity], F32)
        for p in range(1, pairs_per_kv):
            row = jnp.where(pair_of_col == p, sink_ref[kh * Q_PER_KV + 2 * p + parity], row)
        return row

    sinks = [[sink_row(kh, parity) for parity in range(2)] for kh in range(N_KV_HEADS)]

    contract_last = (((1,), (1,)), ((), ()))
    k_placed = {(kh, 0): k_lo[kh] for kh in range(N_KV_HEADS)}
    k_placed.update({(kh, 1): k_hi[kh] for kh in range(N_KV_HEADS)})
    n_blocks = rows // blk
    def group_chains(j0):
        return [(j, kh, parity) for j in range(j0, j0 + ATTN_BLOCKS_PER_GROUP)
                for kh in range(N_KV_HEADS) for parity in range(2)]

    def scores(j0):
        qs = {(j, kh): jnp.concatenate(
            [q[j * blk:(j + 1) * blk,
               (kh * pairs_per_kv + p) * pair:(kh * pairs_per_kv + p + 1) * pair]
             for p in range(pairs_per_kv)], axis=0)
              for j in range(j0, j0 + ATTN_BLOCKS_PER_GROUP)
              for kh in range(N_KV_HEADS)}
        return [lax.dot_general(k_placed[kh, parity][j * blk:(j + 2) * blk, :], qs[j, kh],
                                contract_last, preferred_element_type=F32)
                for j, kh, parity in group_chains(j0)]

    group_starts = list(range(0, n_blocks, ATTN_BLOCKS_PER_GROUP))
    s_ahead = scores(group_starts[0])
    for gi, j0 in enumerate(group_starts):
        chains = group_chains(j0)
        s = s_ahead
        if gi + 1 < len(group_starts):
            s_ahead = scores(group_starts[gi + 1])
        comb = [jnp.where(from_prev, t[:blk, :], t[blk:, :]) for t in s]
        comb = [jnp.where(no_prev, NEG_INF, t) if j == 0 else t
                for t, (j, _, _) in zip(comb, chains)]
        m = [jnp.maximum(jnp.max(t, axis=0, keepdims=True), sinks[kh][parity])
             for t, (_, kh, parity) in zip(comb, chains)]
        e = [jnp.exp(t - mm) for t, mm in zip(comb, m)]
        den = [jnp.sum(t, axis=0, keepdims=True) + jnp.exp(sinks[kh][parity] - mm)
               for t, mm, (_, kh, parity) in zip(e, m, chains)]
        pr = [t * (1.0 / dd) for t, dd in zip(e, den)]
        probs = [jnp.concatenate([jnp.where(from_prev, t, 0.0), jnp.where(from_prev, 0.0, t)],
                                 axis=0).astype(BF16) for t in pr]
        for i in range(0, len(chains), 2):
            j, kh, _ = chains[i]
            t0 = kh * pairs_per_kv
            rhs = jnp.concatenate(probs[i:i + 2], axis=0)
            lhs = jnp.concatenate([vt_lo[kh][:, j * blk:(j + 2) * blk],
                                   vt_hi[kh][:, j * blk:(j + 2) * blk]], axis=1)
            ot = _dot(lhs, rhs)
            for p in range(pairs_per_kv):
                att_ref[(t0 + p) * pair:(t0 + p + 1) * pair, j * blk:(j + 1) * blk] = (
                    ot[:, p * blk:(p + 1) * blk])

        gs = slice(j0 * blk, (j0 + ATTN_BLOCKS_PER_GROUP) * blk)
        att = att_ref[:, gs].T.astype(BF16)
        m = _dot(att, wo_ref[...]) + bo_ref[...]
        o_ref[gs, :] = x_ref[gs, :] + _rms(m, gout_ref[...])


def _attn_block(h, batch, g_in, g_out, w_qkv, b_qkv, sinks, w_o, b_o):
    n, d = h.shape
    qd = w_o.shape[0]
    qkv_w = w_qkv.shape[1]
    tiles = n // batch // MIX_ROWS
    assert n == batch * tiles * MIX_ROWS and MIX_ROWS % ATTN_BLOCK == 0
    assert N_KV_HEADS * HEAD_DIM == V7X_LANES
    row_spec = pl.BlockSpec((MIX_ROWS, d), lambda b, s: (b * tiles + s, 0))
    return pl.pallas_call(
        _attn_kernel,
        name="swa",
        grid=(batch, tiles),
        in_specs=[
            pl.BlockSpec(memory_space=pltpu.SMEM),
            row_spec,
            _const_spec((1, d)),
            _const_spec((1, d)),
            _const_spec((d, qkv_w)),
            _const_spec((1, qkv_w)),
            _const_spec((qd, d)),
            _const_spec((1, d)),
        ],
        out_specs=row_spec,
        out_shape=jax.ShapeDtypeStruct((n, d), F32),
        scratch_shapes=[
            pltpu.VMEM((ATTN_BLOCK, V7X_LANES), F32),
            pltpu.VMEM((ATTN_BLOCK, V7X_LANES), F32),
            pltpu.VMEM((qd, MIX_ROWS), F32),
        ],
        compiler_params=pltpu.CompilerParams(
            dimension_semantics=("arbitrary", "arbitrary"),
            vmem_limit_bytes=V7X_VMEM_LIMIT_BYTES),
    )(sinks, h, g_in.reshape(1, d), g_out.reshape(1, d), w_qkv.astype(BF16),
      b_qkv.reshape(1, qkv_w), w_o.astype(BF16), b_o.reshape(1, d))


def kernel(x, norm_ffn, norm_mix, ffn_w_gate, ffn_w_up, ffn_w_down, lru_w_in, lru_b_in, lru_conv_w, lru_conv_b, lru_w_a, lru_b_a, lru_w_i, lru_b_i, lru_lambda, lru_w_out, lru_b_out, attn_w_qkv, attn_b_qkv, attn_sinks, attn_w_o, attn_b_o):
    batch, seq, d = x.shape
    depth = norm_ffn.shape[0]
    h = x.reshape(batch * seq, d)
    for layer in range(depth):
        h = _ffn_block(h, norm_ffn[layer, 0, 0], norm_ffn[layer, 0, 1],
                       ffn_w_gate, ffn_w_up, ffn_w_down, layer, 0)
        j = layer // 2
        if layer % 2 == 0:
            h = _lru_block(h, batch, norm_mix[layer, 0], norm_mix[layer, 1],
                           lru_w_in[j], lru_b_in[j], lru_conv_w[j], lru_conv_b[j],
                           lru_w_a[j], lru_b_a[j], lru_w_i[j], lru_b_i[j],
                           lru_lambda[j], lru_w_out[j], lru_b_out[j])
        else:
            h = _attn_block(h, batch, norm_mix[layer, 0], norm_mix[layer, 1],
                            attn_w_qkv[j], attn_b_qkv[j], attn_sinks[j],
                            attn_w_o[j], attn_b_o[j])
        h = _ffn_block(h, norm_ffn[layer, 1, 0], norm_ffn[layer, 1, 1],
                       ffn_w_gate, ffn_w_up, ffn_w_down, layer, 1)
    return h.reshape(batch, seq, d)
```

```python
import functools
import math

import jax
import jax.numpy as jnp
from jax import lax
from jax.experimental import pallas as pl
from jax.experimental.pallas import tpu as pltpu

F32 = jnp.float32
BF16 = jnp.bfloat16

RMS_EPS = 1e-6
LOG2_E = math.log2(math.e)
NEG_INF = -1e30
LRU_C = 8.0
LRU_HEADS = 4
CONV_WIDTH = 4
HEAD_DIM = 64
N_KV_HEADS = 2
Q_PER_KV = 8
ATTN_BLOCK = 128

V7X_SUBLANES = 8
V7X_LANES = 128
V7X_MXU_DIM = 256
V7X_VMEM_LIMIT_BYTES = 56 * 1024 * 1024

FFN_ROWS = 1024
FFN_CHUNK = V7X_MXU_DIM
FFN_STAGE_ROWS = 128
FFN_STAGE_SLOTS = 4
MIX_ROWS = 512
LRU_ROWS = 1024
ATTN_BLOCKS_PER_GROUP = 2


def _rms(x, g):
    ms = jnp.mean(x * x, axis=-1, keepdims=True)
    return x * lax.rsqrt(ms + RMS_EPS) * g


def _dot(a, b):
    return jnp.dot(a, b, preferred_element_type=F32)


def _sigmoid(v):
    return 1.0 / (1.0 + jnp.exp2(-LOG2_E * v))


def _gelu_tanh(v):
    k1 = -2.0 * math.sqrt(2.0 / math.pi) * LOG2_E
    t = v * (k1 + (k1 * 0.044715) * (v * v))
    return v / (1.0 + jnp.exp2(t))


def _const_spec(shape):
    nd = len(shape)
    return pl.BlockSpec(shape, lambda *_: (0,) * nd)


def _load_cast(jobs):
    tasks = []
    used = {}
    for src_ref, dst_ref, stage_ref, sem_ref in jobs:
        slots, rows = stage_ref.shape[:2]
        for k in range(src_ref.shape[0] // rows):
            slot = used.get(id(stage_ref), 0) % slots
            used[id(stage_ref)] = used.get(id(stage_ref), 0) + 1
            copy = functools.partial(
                pltpu.make_async_copy, src_ref.at[pl.ds(k * rows, rows), :],
                stage_ref.at[slot], sem_ref.at[slot])
            tasks.append((copy, stage_ref, slot, dst_ref, pl.ds(k * rows, rows)))

    ahead = FFN_STAGE_SLOTS - 1
    for copy, *_ in tasks[:ahead]:
        copy().start()
    for t, (copy, stage_ref, slot, dst_ref, rows) in enumerate(tasks):
        if t + ahead < len(tasks):
            tasks[t + ahead][0]().start()
        copy().wait()
        dst_ref[rows, :] = stage_ref[slot].astype(BF16)


def _ffn_kernel(x_ref, gin_ref, gout_ref, wg_hbm, wu_hbm, wd_hbm, o_ref,
                mid_ref, wg_ref, wu_ref, wd_ref, stage_in_ref, stage_out_ref, sem_in_ref,
                sem_out_ref, *, layer, which):
    @pl.when(pl.program_id(0) == 0)
    def _():
        _load_cast([(wg_hbm.at[layer, which], wg_ref, stage_in_ref, sem_in_ref),
                    (wu_hbm.at[layer, which], wu_ref, stage_in_ref, sem_in_ref),
                    (wd_hbm.at[layer, which], wd_ref, stage_out_ref, sem_out_ref)])

    x = x_ref[...]
    xn = _rms(x, gin_ref[...]).astype(BF16)
    d_ff = wg_ref.shape[1]
    for c in range(d_ff // FFN_CHUNK):
        sl = slice(c * FFN_CHUNK, (c + 1) * FFN_CHUNK)
        g = _dot(xn, wg_ref[:, sl])
        u = _dot(xn, wu_ref[:, sl])
        mid_ref[:, sl] = (g * jax.nn.sigmoid(g) * u).astype(BF16)
    f = _dot(mid_ref[...], wd_ref[...])
    o_ref[...] = x + _rms(f, 0.5 * gout_ref[...])


def _ffn_block(h, g_in, g_out, w_gate, w_up, w_down, layer, which):
    n, d = h.shape
    d_ff = w_gate.shape[-1]
    assert n % FFN_ROWS == 0 and d_ff % FFN_CHUNK == 0
    assert d % FFN_STAGE_ROWS == 0 and d_ff % FFN_STAGE_ROWS == 0
    row_spec = pl.BlockSpec((FFN_ROWS, d), lambda i: (i, 0))
    hbm_spec = pl.BlockSpec(memory_space=pl.ANY)
    return pl.pallas_call(
        functools.partial(_ffn_kernel, layer=layer, which=which),
        name="ffn",
        grid=(n // FFN_ROWS,),
        in_specs=[
            row_spec,
            _const_spec((1, d)),
            _const_spec((1, d)),
            hbm_spec,
            hbm_spec,
            hbm_spec,
        ],
        out_specs=row_spec,
        out_shape=jax.ShapeDtypeStruct((n, d), F32),
        scratch_shapes=[
            pltpu.VMEM((FFN_ROWS, d_ff), BF16),
            pltpu.VMEM((d, d_ff), BF16),
            pltpu.VMEM((d, d_ff), BF16),
            pltpu.VMEM((d_ff, d), BF16),
            pltpu.VMEM((FFN_STAGE_SLOTS, FFN_STAGE_ROWS, d_ff), F32),
            pltpu.VMEM((FFN_STAGE_SLOTS, FFN_STAGE_ROWS, d), F32),
            pltpu.SemaphoreType.DMA((FFN_STAGE_SLOTS,)),
            pltpu.SemaphoreType.DMA((FFN_STAGE_SLOTS,)),
        ],
        compiler_params=pltpu.CompilerParams(
            dimension_semantics=("arbitrary",),
            vmem_limit_bytes=V7X_VMEM_LIMIT_BYTES),
    )(h, g_in.reshape(1, d), g_out.reshape(1, d), w_gate, w_up, w_down)


def _scan_step(a, b, shift, idx):
    keep = idx >= shift
    a_prev = pltpu.roll(a, shift, axis=0)
    b_prev = pltpu.roll(b, shift, axis=0)
    b = jnp.where(keep, a * b_prev, 0.0) + b
    a = jnp.where(keep, a * a_prev, a)
    return a, b


def _lru_kernel(x_ref, gin_ref, gout_ref, win_ref, bin_ref, cw_ref, cb_ref, wai_ref,
                bai_ref, lam_ref, wout_ref, bout_ref, o_ref,
                xcp_ref, hs_ref, tail_ref, hcar_ref, a_ref, b_ref):
    strips, rows, lanes = a_ref.shape
    nseg = V7X_SUBLANES
    seg = rows // nseg
    pitch = xcp_ref.shape[1] // nseg
    width = strips * lanes
    blk = width // LRU_HEADS
    taps = CONV_WIDTH - 1

    @pl.when(pl.program_id(1) == 0)
    def _():
        tail_ref[...] = jnp.zeros_like(tail_ref)
        hcar_ref[...] = jnp.zeros_like(hcar_ref)

    x = x_ref[...]
    xn = _rms(x, gin_ref[...]).astype(BF16)
    proj = _dot(xn, win_ref[...]) + bin_ref[...]
    y = _gelu_tanh(proj[:, :width])

    for c in range(strips):
        for s in range(nseg):
            xcp_ref[c, s * pitch:s * pitch + seg, :] = (
                proj[s * seg:(s + 1) * seg, width + c * lanes:width + (c + 1) * lanes])

    sub = lax.broadcasted_iota(jnp.int32, (nseg, lanes), 0)
    xb_strips = []
    for c in range(strips):
        ls = slice(c * lanes, (c + 1) * lanes)
        regs = [xcp_ref[c, pl.ds(j, nseg, stride=pitch), :] for j in range(seg)]
        head = []
        for k in range(taps):
            last = regs[seg - taps + k]
            prev_tile = tail_ref[c, k * nseg:(k + 1) * nseg, :]
            head.append(pltpu.roll(jnp.where(sub == nseg - 1, prev_tile, last), 1, axis=0))
            tail_ref[c, k * nseg:(k + 1) * nseg, :] = last
        ext = head + regs
        w = [cw_ref[k:k + 1, ls] for k in range(CONV_WIDTH)]
        cb = cb_ref[:, ls]
        out = []
        for j in range(seg):
            acc = cb + w[taps] * ext[j + taps]
            for k in range(taps):
                acc = acc + w[k] * ext[j + k]
            out.append(acc)
        xb_strips.append(jnp.concatenate(out, axis=0))
    xb = jnp.concatenate(xb_strips, axis=1)

    lam = lam_ref[...]
    log_sig_lam = jnp.minimum(lam, 0.0) - jnp.log(1.0 + jnp.exp(-jnp.abs(lam)))
    log2_a_scale = (LRU_C * LOG2_E) * log_sig_lam
    for h in range(LRU_HEADS):
        cs = slice(h * blk, (h + 1) * blk)
        xbh = xb[:, cs]
        pre = _dot(xbh.astype(BF16), wai_ref[h]) + bai_ref[h]
        for t in range(blk // lanes):
            cc = h * (blk // lanes) + t
            ts = slice(t * lanes, (t + 1) * lanes)
            r = _sigmoid(pre[:, ts])
            i_gate = _sigmoid(pre[:, blk + t * lanes:blk + (t + 1) * lanes])
            a = jnp.exp2(r * log2_a_scale[:, cc * lanes:(cc + 1) * lanes])
            z = 1.0 - a * a
            mult = jnp.where(z > 0.0, z * lax.rsqrt(z), 0.0)
            a_ref[cc] = a
            b_ref[cc] = mult * (i_gate * xbh[:, ts])

    for c in range(strips):
        ls = slice(c * lanes, (c + 1) * lanes)
        hz = []
        az = []
        for j in range(seg):
            a_j = a_ref[c, j * nseg:(j + 1) * nseg, :]
            b_j = b_ref[c, j * nseg:(j + 1) * nseg, :]
            hz.append(b_j if j == 0 else a_j * hz[-1] + b_j)
            az.append(a_j if j == 0 else a_j * az[-1])
        a_tot, h_tot = az[-1], hz[-1]
        shift = 1
        while shift < nseg:
            a_tot, h_tot = _scan_step(a_tot, h_tot, shift, sub)
            shift *= 2
        h_in = hcar_ref[:, ls]
        seg_out = a_tot * h_in + h_tot
        hcar_ref[:, ls] = seg_out[nseg - 1:nseg, :]
        seg_in = jnp.where(sub >= 1, pltpu.roll(seg_out, 1, axis=0), h_in)
        for j in range(seg):
            hs_ref[c, pl.ds(j, nseg, stride=pitch), :] = hz[j] + az[j] * seg_in

    hseq = jnp.concatenate(
        [jnp.concatenate([hs_ref[c, s * pitch:s * pitch + seg, :] for s in range(nseg)],
                         axis=0) for c in range(strips)], axis=1)
    m = _dot((hseq * y).astype(BF16), wout_ref[...]) + bout_ref[...]
    o_ref[...] = x + _rms(m, gout_ref[...])


def _lru_block(h, batch, g_in, g_out, w_in, b_in, conv_w, conv_b, w_a, b_a, w_i, b_i,
               lam, w_out, b_out):
    n, d = h.shape
    width = w_out.shape[0]
    blk = width // LRU_HEADS
    tiles = n // batch // LRU_ROWS
    assert n == batch * tiles * LRU_ROWS
    strips = width // V7X_LANES
    seg_tiles = LRU_ROWS // V7X_SUBLANES // V7X_SUBLANES
    padded_rows = V7X_SUBLANES * V7X_SUBLANES * (seg_tiles + 1 - seg_tiles % 2)
    w_ai = jnp.concatenate([w_a, w_i], axis=-1).astype(BF16)
    b_ai = jnp.concatenate([b_a.reshape(LRU_HEADS, 1, blk),
                            b_i.reshape(LRU_HEADS, 1, blk)], axis=-1)
    row_spec = pl.BlockSpec((LRU_ROWS, d), lambda b, s: (b * tiles + s, 0))
    return pl.pallas_call(
        _lru_kernel,
        name="rglru",
        grid=(batch, tiles),
        in_specs=[
            row_spec,
            _const_spec((1, d)),
            _const_spec((1, d)),
            _const_spec((d, 2 * width)),
            _const_spec((1, 2 * width)),
            _const_spec((CONV_WIDTH, width)),
            _const_spec((1, width)),
            _const_spec((LRU_HEADS, blk, 2 * blk)),
            _const_spec((LRU_HEADS, 1, 2 * blk)),
            _const_spec((1, width)),
            _const_spec((width, d)),
            _const_spec((1, d)),
        ],
        out_specs=row_spec,
        out_shape=jax.ShapeDtypeStruct((n, d), F32),
        scratch_shapes=[
            pltpu.VMEM((strips, padded_rows, V7X_LANES), F32),
            pltpu.VMEM((strips, padded_rows, V7X_LANES), F32),
            pltpu.VMEM((strips, (CONV_WIDTH - 1) * V7X_SUBLANES, V7X_LANES), F32),
            pltpu.VMEM((1, width), F32),
            pltpu.VMEM((strips, LRU_ROWS, V7X_LANES), F32),
            pltpu.VMEM((strips, LRU_ROWS, V7X_LANES), F32),
        ],
        compiler_params=pltpu.CompilerParams(
            dimension_semantics=("arbitrary", "arbitrary"),
            vmem_limit_bytes=V7X_VMEM_LIMIT_BYTES),
    )(h, g_in.reshape(1, d), g_out.reshape(1, d), w_in.astype(BF16),
      b_in.reshape(1, -1), conv_w, conv_b.reshape(1, -1), w_ai, b_ai,
      lam.reshape(1, -1), w_out.astype(BF16), b_out.reshape(1, d))


def _attn_kernel(sink_ref, x_ref, gin_ref, gout_ref, wqkv_ref, bqkv_ref, wo_ref, bo_ref,
                 o_ref, kprev_ref, vprev_ref, att_ref):
    qd, rows = att_ref.shape
    blk = ATTN_BLOCK
    pair = 2 * HEAD_DIM
    pairs_per_kv = Q_PER_KV // 2
    first_tile = pl.program_id(1) == 0

    @pl.when(first_tile)
    def _():
        kprev_ref[...] = jnp.zeros_like(kprev_ref)
        vprev_ref[...] = jnp.zeros_like(vprev_ref)

    x = x_ref[...]
    xn = _rms(x, gin_ref[...]).astype(BF16)
    kv = _dot(xn, wqkv_ref[:, qd:]) + bqkv_ref[:, qd:]
    q = (_dot(xn, wqkv_ref[:, :qd]) + bqkv_ref[:, :qd]).astype(BF16)
    k_new = kv[:, :pair] * (1.0 / math.sqrt(HEAD_DIM))
    v_new = kv[:, pair:]
    k_all = jnp.concatenate([kprev_ref[...], k_new], axis=0)
    v_all = jnp.concatenate([vprev_ref[...], v_new], axis=0)
    kprev_ref[...] = k_new[rows - blk:, :]
    vprev_ref[...] = v_new[rows - blk:, :]

    def placed(t, axis):
        low = lax.broadcasted_iota(jnp.int32, t.shape, axis) < HEAD_DIM
        swapped = pltpu.roll(t, HEAD_DIM, axis=axis)
        lo = [jnp.where(low, t, 0.0), jnp.where(low, swapped, 0.0)]
        hi = [jnp.where(low, 0.0, swapped), jnp.where(low, 0.0, t)]
        return ([a.astype(BF16) for a in lo], [a.astype(BF16) for a in hi])

    k_lo, k_hi = placed(k_all, 1)
    vt_lo, vt_hi = placed(v_all.T, 0)

    cols = pairs_per_kv * blk
    kj = lax.broadcasted_iota(jnp.int32, (blk, cols), 0)
    qi = jnp.bitwise_and(lax.broadcasted_iota(jnp.int32, (blk, cols), 1), blk - 1)
    from_prev = kj > qi
    no_prev = jnp.logical_and(from_prev, first_tile)
    pair_of_col = lax.broadcasted_iota(jnp.int32, (1, cols), 1) // blk

    def sink_row(kh, parity):
        row = jnp.full((1, cols), sink_ref[kh * Q_PER_KV + parity], F32)
        for p in range(1, pairs_per_kv):
            row = jnp.where(pair_of_col == p, sink_ref[kh * Q_PER_KV + 2 * p + parity], row)
        return row

    sinks = [[sink_row(kh, parity) for parity in range(2)] for kh in range(N_KV_HEADS)]

    contract_last = (((1,), (1,)), ((), ()))
    k_placed = {(kh, 0): k_lo[kh] for kh in range(N_KV_HEADS)}
    k_placed.update({(kh, 1): k_hi[kh] for kh in range(N_KV_HEADS)})
    n_blocks = rows // blk
    def group_chains(j0):
        return [(j, kh, parity) for j in range(j0, j0 + ATTN_BLOCKS_PER_GROUP)
                for kh in range(N_KV_HEADS) for parity in range(2)]

    def scores(j0):
        qs = {(j, kh): jnp.concatenate(
            [q[j * blk:(j + 1) * blk,
               (kh * pairs_per_kv + p) * pair:(kh * pairs_per_kv + p + 1) * pair]
             for p in range(pairs_per_kv)], axis=0)
              for j in range(j0, j0 + ATTN_BLOCKS_PER_GROUP)
              for kh in range(N_KV_HEADS)}
        return [lax.dot_general(k_placed[kh, parity][j * blk:(j + 2) * blk, :], qs[j, kh],
                                contract_last, preferred_element_type=F32)
                for j, kh, parity in group_chains(j0)]

    group_starts = list(range(0, n_blocks, ATTN_BLOCKS_PER_GROUP))
    s_ahead = scores(group_starts[0])
    for gi, j0 in enumerate(group_starts):
        chains = group_chains(j0)
        s = s_ahead
        if gi + 1 < len(group_starts):
            s_ahead = scores(group_starts[gi + 1])
        comb = [jnp.where(from_prev, t[:blk, :], t[blk:, :]) for t in s]
        comb = [jnp.where(no_prev, NEG_INF, t) if j == 0 else t
                for t, (j, _, _) in zip(comb, chains)]
        m = [jnp.maximum(jnp.max(t, axis=0, keepdims=True), sinks[kh][parity])
             for t, (_, kh, parity) in zip(comb, chains)]
        e = [jnp.exp(t - mm) for t, mm in zip(comb, m)]
        den = [jnp.sum(t, axis=0, keepdims=True) + jnp.exp(sinks[kh][parity] - mm)
               for t, mm, (_, kh, parity) in zip(e, m, chains)]
        pr = [t * (1.0 / dd) for t, dd in zip(e, den)]
        probs = [jnp.concatenate([jnp.where(from_prev, t, 0.0), jnp.where(from_prev, 0.0, t)],
                                 axis=0).astype(BF16) for t in pr]
        for i in range(0, len(chains), 2):
            j, kh, _ = chains[i]
            t0 = kh * pairs_per_kv
            rhs = jnp.concatenate(probs[i:i + 2], axis=0)
            lhs = jnp.concatenate([vt_lo[kh][:, j * blk:(j + 2) * blk],
                                   vt_hi[kh][:, j * blk:(j + 2) * blk]], axis=1)
            ot = _dot(lhs, rhs)
            for p in range(pairs_per_kv):
                att_ref[(t0 + p) * pair:(t0 + p + 1) * pair, j * blk:(j + 1) * blk] = (
                    ot[:, p * blk:(p + 1) * blk])

        gs = slice(j0 * blk, (j0 + ATTN_BLOCKS_PER_GROUP) * blk)
        att = att_ref[:, gs].T.astype(BF16)
        m = _dot(att, wo_ref[...]) + bo_ref[...]
        o_ref[gs, :] = x_ref[gs, :] + _rms(m, gout_ref[...])


def _attn_block(h, batch, g_in, g_out, w_qkv, b_qkv, sinks, w_o, b_o):
    n, d = h.shape
    qd = w_o.shape[0]
    qkv_w = w_qkv.shape[1]
    tiles = n // batch // MIX_ROWS
    assert n == batch * tiles * MIX_ROWS and MIX_ROWS % ATTN_BLOCK == 0
    assert N_KV_HEADS * HEAD_DIM == V7X_LANES
    row_spec = pl.BlockSpec((MIX_ROWS, d), lambda b, s: (b * tiles + s, 0))
    return pl.pallas_call(
        _attn_kernel,
        name="swa",
        grid=(batch, tiles),
        in_specs=[
            pl.BlockSpec(memory_space=pltpu.SMEM),
            row_spec,
            _const_spec((1, d)),
            _const_spec((1, d)),
            _const_spec((d, qkv_w)),
            _const_spec((1, qkv_w)),
            _const_spec((qd, d)),
            _const_spec((1, d)),
        ],
        out_specs=row_spec,
        out_shape=jax.ShapeDtypeStruct((n, d), F32),
        scratch_shapes=[
            pltpu.VMEM((ATTN_BLOCK, V7X_LANES), F32),
            pltpu.VMEM((ATTN_BLOCK, V7X_LANES), F32),
            pltpu.VMEM((qd, MIX_ROWS), F32),
        ],
        compiler_params=pltpu.CompilerParams(
            dimension_semantics=("arbitrary", "arbitrary"),
            vmem_limit_bytes=V7X_VMEM_LIMIT_BYTES),
    )(sinks, h, g_in.reshape(1, d), g_out.reshape(1, d), w_qkv.astype(BF16),
      b_qkv.reshape(1, qkv_w), w_o.astype(BF16), b_o.reshape(1, d))


def kernel(x, norm_ffn, norm_mix, ffn_w_gate, ffn_w_up, ffn_w_down, lru_w_in, lru_b_in, lru_conv_w, lru_conv_b, lru_w_a, lru_b_a, lru_w_i, lru_b_i, lru_lambda, lru_w_out, lru_b_out, attn_w_qkv, attn_b_qkv, attn_sinks, attn_w_o, attn_b_o):
    batch, seq, d = x.shape
    depth = norm_ffn.shape[0]
    h = x.reshape(batch * seq, d)
    for layer in range(depth):
        h = _ffn_block(h, norm_ffn[layer, 0, 0], norm_ffn[layer, 0, 1],
                       ffn_w_gate, ffn_w_up, ffn_w_down, layer, 0)
        j = layer // 2
        if layer % 2 == 0:
            h = _lru_block(h, batch, norm_mix[layer, 0], norm_mix[layer, 1],
                           lru_w_in[j], lru_b_in[j], lru_conv_w[j], lru_conv_b[j],
                           lru_w_a[j], lru_b_a[j], lru_w_i[j], lru_b_i[j],
                           lru_lambda[j], lru_w_out[j], lru_b_out[j])
        else:
            h = _attn_block(h, batch, norm_mix[layer, 0], norm_mix[layer, 1],
                            attn_w_qkv[j], attn_b_qkv[j], attn_sinks[j],
                            attn_w_o[j], attn_b_o[j])
        h = _ffn_block(h, norm_ffn[layer, 1, 0], norm_ffn[layer, 1, 1],
                       ffn_w_gate, ffn_w_up, ffn_w_down, layer, 1)
    return h.reshape(batch, seq, d)
```
